```python
import math
import jax, jax.numpy as jnp
from jax import lax
import numpy as np

D_MODEL = 1024
BATCH = 32
SEQ = 256
DEPTH = 2
DEC_BATCH = 8
DEC_SEQ = 1024
PAST_LEN = 512

GRID_W = 64
HEAD_DIM = 64
W_A = D_MODEL // 4
W_B = D_MODEL // 2
W_C = D_MODEL - W_A - W_B
H_A = W_A // HEAD_DIM
HD_A = HEAD_DIM
RWKV_DECAY_LORA = 64
RWKV_ICLR_LORA = 64
RWKV_GATE_LORA = 128
DECAY_SCALE = 0.606531
H_B = W_B // HEAD_DIM
DV = HEAD_DIM
DK = HEAD_DIM // 2
ROPE_PAIRS = DK // 4
ROPE_BASE = 10000.0
Q_BLOCK = 128
LRU_BW = HEAD_DIM
LRU_BLOCKS = W_C // LRU_BW
CONV_W = 4
CONV_LEFT = 2
LRU_C = 8.0
PEER_HEADS = 8
PEER_N_KEYS = 128
PEER_TOPK = 16
PEER_DQ = 256
N_EXPERTS = PEER_N_KEYS ** 2
PEER_BLOCK = 128
D_IN = 3 * W_A + 3 * W_B + 2 * W_C
SPLIT_POINTS = (W_A, 2 * W_A, 3 * W_A, 3 * W_A + W_B, 3 * W_A + 2 * W_B, 3 * W_A + 3 * W_B, 3 * W_A + 3 * W_B + W_C)
RMS_EPS = 1e-6
GN_EPS = 64e-5

kernel_name = 'hybrid_diffusion_rwkv7_diffattn_rglru_peer_step'


def rmsnorm(x, g, eps=RMS_EPS):
    xf = x.astype(jnp.float32)
    y = xf * lax.rsqrt(jnp.mean(xf * xf, axis=-1, keepdims=True) + eps)
    return (y * g.astype(jnp.float32)).astype(x.dtype)


def head_layernorm(x, g, eps=GN_EPS):
    xf = x.astype(jnp.float32)
    mu = jnp.mean(xf, axis=-1, keepdims=True)
    var = jnp.mean(jnp.square(xf - mu), axis=-1, keepdims=True)
    return ((xf - mu) * lax.rsqrt(var + eps) * g.astype(jnp.float32)).astype(x.dtype)


def l2_normalize(x):
    xf = x.astype(jnp.float32)
    return (xf * lax.rsqrt(jnp.maximum(jnp.sum(xf * xf, axis=-1, keepdims=True), 1e-12))).astype(x.dtype)


def adaln_params(cond, w, b):
    m = jax.nn.silu(cond) @ w + b
    return jnp.split(m[:, None, :], 6, axis=-1)


def axial_rope_tables(T):
    rows = T // GRID_W
    row = jnp.repeat(jnp.arange(rows), GRID_W).astype(jnp.float32)
    col = jnp.tile(jnp.arange(GRID_W), rows).astype(jnp.float32)
    inv = ROPE_BASE ** (-jnp.arange(ROPE_PAIRS, dtype=jnp.float32) / ROPE_PAIRS)
    ang = jnp.concatenate([row[:, None] * inv, col[:, None] * inv], axis=-1)
    return jnp.cos(ang), jnp.sin(ang)


def apply_axial_rope(x, cos, sin):
    x1 = x[..., 0::2]
    x2 = x[..., 1::2]
    c = cos[None, :, None, None, :].astype(x.dtype)
    s = sin[None, :, None, None, :].astype(x.dtype)
    return jnp.stack([x1 * c - x2 * s, x1 * s + x2 * c], axis=-1).reshape(x.shape)


def blocked_diff_attention(q, k, v, lam):
    B, H, Sq = q.shape[:3]
    nb = Sq // Q_BLOCK
    qb = jnp.moveaxis(q.reshape(B, H, nb, Q_BLOCK, 2, DK), 2, 0)
    scale = DK ** -0.5

    def one_block(qi):
        s = jnp.einsum('bhqcd,bhkcd->bhcqk', qi, k).astype(jnp.float32) * scale
        p = jax.nn.softmax(s, axis=-1)
        p = p[:, :, 0] - lam * p[:, :, 1]
        return jnp.einsum('bhqk,bhkd->bhqd', p.astype(v.dtype), v)

    o = lax.map(one_block, qb)
    return jnp.moveaxis(o, 0, 2).reshape(B, H, Sq, DV)


def rwkv7_scan(r, w, k, v, kk, a, s0, reverse):
    xs = (jnp.swapaxes(r, 0, 1), jnp.swapaxes(w, 0, 1), jnp.swapaxes(k, 0, 1),
          jnp.swapaxes(v, 0, 1), jnp.swapaxes(kk, 0, 1), jnp.swapaxes(a, 0, 1))

    def step(S, inp):
        r_t, w_t, k_t, v_t, kk_t, a_t = inp
        sa = jnp.einsum('bhij,bhj->bhi', S, -kk_t)
        S = (S * w_t[:, :, None, :] + sa[..., None] * (kk_t * a_t)[:, :, None, :]
             + v_t[..., None] * k_t[:, :, None, :])
        return S, jnp.einsum('bhij,bhj->bhi', S, r_t)

    s_last, o = lax.scan(step, s0, xs, reverse=reverse)
    return jnp.swapaxes(o, 0, 1), s_last


def rwkv7_mixer(h, r, k, v, lp, s0):
    B, T, _ = h.shape
    shp = (B, T, H_A, HD_A)
    rh = r.reshape(shp)
    kh = k.reshape(shp)
    vh = v.reshape(shp)
    kk = l2_normalize((k * lp['rwkv_kk']).reshape(shp))
    o_sum = None
    states = []
    for d in range(2):
        w = jnp.exp(-DECAY_SCALE * jax.nn.sigmoid(lp['rwkv_w0'][d] + jnp.tanh(h @ lp['rwkv_wA'][d]) @ lp['rwkv_wB'][d]))
        a = jax.nn.sigmoid(lp['rwkv_a0'][d] + (h @ lp['rwkv_aA'][d]) @ lp['rwkv_aB'][d])
        kd = k * (1.0 + (a - 1.0) * lp['rwkv_ka'])
        o, s_last = rwkv7_scan(rh, w.reshape(shp), kd.reshape(shp), vh, kk, a.reshape(shp), s0[:, d], reverse=(d == 1))
        o_sum = o if o_sum is None else o_sum + o
        states.append(s_last)
    o = head_layernorm(o_sum, lp['rwkv_ln_g'].reshape(H_A, HD_A))
    bonus = jnp.sum(rh * kh * lp['rwkv_rk'], axis=-1, keepdims=True) * vh
    g = jax.nn.sigmoid(h @ lp['rwkv_gA']) @ lp['rwkv_gB']
    out = (o + bonus).reshape(B, T, W_A) * g
    return out, jnp.stack(states, axis=1)


def diff_attention_mixer(q, k, v, lp, lam_init, rope, ctx_k, ctx_v):
    B, T, _ = q.shape
    qh = q.reshape(B, T, H_B, 2, DK)
    kh = k.reshape(B, T, H_B, 2, DK)
    vh = v.reshape(B, T, H_B, DV)
    if rope is not None:
        cos, sin = rope
        qh = apply_axial_rope(qh, cos, sin)
        kh = apply_axial_rope(kh, cos, sin)
    k_new = jnp.transpose(kh.reshape(B, T, H_B, 2 * DK), (0, 2, 1, 3))
    v_new = jnp.transpose(vh, (0, 2, 1, 3))
    if ctx_k is None:
        k_all, v_all = k_new, v_new
    else:
        k_all = jnp.concatenate([ctx_k, k_new], axis=2)
        v_all = jnp.concatenate([ctx_v, v_new], axis=2)
    Sk = k_all.shape[2]
    f32 = jnp.float32
    lam = (jnp.exp(jnp.sum(lp['diff_lq1'].astype(f32) * lp['diff_lk1'].astype(f32)))
           - jnp.exp(jnp.sum(lp['diff_lq2'].astype(f32) * lp['diff_lk2'].astype(f32))) + lam_init)
    o = blocked_diff_attention(jnp.transpose(qh, (0, 2, 1, 3, 4)), k_all.reshape(B, H_B, Sk, 2, DK), v_all, lam)
    o = rmsnorm(o, lp['diff_subln_g'], eps=1e-5) * (1.0 - lam_init)
    out = jnp.transpose(o, (0, 2, 1, 3)).reshape(B, T, W_B)
    return out, k_new, v_new


def centred_depthwise_conv(x, w, b):
    T = x.shape[1]
    xp = jnp.pad(x, ((0, 0), (CONV_LEFT, CONV_W - 1 - CONV_LEFT), (0, 0)))
    y = b
    for j in range(CONV_W):
        y = y + xp[:, j:j + T] * w[j]
    return y


def linear_scan(a, b, h0, reverse):
    def step(h, ab):
        h = ab[0] * h + ab[1]
        return h, h
    h_last, hs = lax.scan(step, h0, (jnp.swapaxes(a, 0, 1), jnp.swapaxes(b, 0, 1)), reverse=reverse)
    return jnp.swapaxes(hs, 0, 1), h_last


def rglru_mixer(xc, gc, lp, s0):
    B, T, _ = xc.shape
    xconv = centred_depthwise_conv(xc, lp['lru_conv_w'], lp['lru_conv_b'])
    xg = xconv.reshape(B, T, LRU_BLOCKS, LRU_BW)
    h_sum = None
    states = []
    for d in range(2):
        gate_r = jax.nn.sigmoid(jnp.einsum('btgi,gij->btgj', xg, lp['lru_wa'][d]).reshape(B, T, W_C) + lp['lru_ba'][d])
        gate_i = jax.nn.sigmoid(jnp.einsum('btgi,gij->btgj', xg, lp['lru_wx'][d]).reshape(B, T, W_C) + lp['lru_bx'][d])
        log_a = -LRU_C * gate_r.astype(jnp.float32) * jax.nn.softplus(-lp['lru_lambda'][d].astype(jnp.float32))
        a = jnp.exp(log_a).astype(xc.dtype)
        bterm = (jnp.sqrt(-jnp.expm1(2.0 * log_a)) * (gate_i * xconv).astype(jnp.float32)).astype(xc.dtype)
        hs, h_last = linear_scan(a, bterm, s0[:, d], reverse=(d == 1))
        h_sum = hs if h_sum is None else h_sum + hs
        states.append(h_last)
    y = h_sum * jax.nn.gelu(gc, approximate=False)
    return y, jnp.stack(states, axis=1)


def token_mixers(h, lp, lam_init, rope, ctx_k, ctx_v, s_rwkv0, s_lru0):
    proj = h @ lp['w_in']
    r_a, k_a, v_a, q_b, k_b, v_b, x_c, g_c = jnp.split(proj, SPLIT_POINTS, axis=-1)
    out_a, s_rwkv = rwkv7_mixer(h, r_a, k_a, v_a, lp, s_rwkv0)
    out_b, k_new, v_new = diff_attention_mixer(q_b, k_b, v_b, lp, lam_init, rope, ctx_k, ctx_v)
    out_c, s_lru = rglru_mixer(x_c, g_c, lp, s_lru0)
    y = jnp.concatenate([out_a, out_b, out_c], axis=-1) @ lp['w_out']
    return y, k_new, v_new, s_rwkv, s_lru


def peer_ffn(h, wq, sub_keys, u_tab, v_tab):
    B, T, D = h.shape
    n = B * T
    x = h.reshape(n, D)
    q = (x @ wq).reshape(n, PEER_HEADS, 2, PEER_DQ // 2)
    s = jnp.einsum('nhcd,hckd->nhck', q, sub_keys).astype(jnp.float32)
    s1, i1 = lax.top_k(s[:, :, 0], PEER_TOPK)
    s2, i2 = lax.top_k(s[:, :, 1], PEER_TOPK)
    cand = (s1[..., :, None] + s2[..., None, :]).reshape(n, PEER_HEADS, PEER_TOPK * PEER_TOPK)
    cidx = (i1[..., :, None] * PEER_N_KEYS + i2[..., None, :]).reshape(n, PEER_HEADS, PEER_TOPK * PEER_TOPK)
    top, pos = lax.top_k(cand, PEER_TOPK)
    idx = jnp.take_along_axis(cidx, pos, axis=-1)
    gate = jax.nn.softmax(top, axis=-1).astype(h.dtype)
    nb = n // PEER_BLOCK

    def one_block(args):
        xb, ib, gb = args
        act = jax.nn.gelu(jnp.einsum('nd,nhkd->nhk', xb, u_tab[ib]), approximate=False)
        return jnp.einsum('nhk,nhkd->nd', gb * act, v_tab[ib])

    y = lax.map(one_block, (x.reshape(nb, PEER_BLOCK, D),
                            idx.reshape(nb, PEER_BLOCK, PEER_HEADS, PEER_TOPK),
                            gate.reshape(nb, PEER_BLOCK, PEER_HEADS, PEER_TOPK)))
    return y.reshape(B, T, D)


def trunk_layer(x, mod, lp, lam_init, rope, ctx_k, ctx_v, s_rwkv0, s_lru0):
    sh1, sc1, g1, sh2, sc2, g2 = mod
    h = rmsnorm(x, lp['norm1_g']) * (1.0 + sc1) + sh1
    y, k_new, v_new, s_rwkv, s_lru = token_mixers(h, lp, lam_init, rope, ctx_k, ctx_v, s_rwkv0, s_lru0)
    x = x + g1 * y
    h2 = rmsnorm(x, lp['norm2_g']) * (1.0 + sc2) + sh2
    x = x + g2 * peer_ffn(h2, lp['peer_wq'], lp['peer_keys'], lp['peer_u'], lp['peer_v'])
    return x, k_new, v_new, s_rwkv, s_lru


def setup_inputs(seed: int = 0) -> dict:
    key = jax.random.key(seed)
    ks = iter(jax.random.split(key, 64))
    f32 = jnp.float32

    def nrm(shape, scale):
        return jax.random.normal(next(ks), shape, f32) * scale

    def gain(shape):
        return 1.0 + nrm(shape, 0.02)

    u_lam = jax.random.uniform(next(ks), (DEPTH, 2, W_C), f32, 0.9, 0.999)
    p_lam = u_lam ** (1.0 / LRU_C)
    lru_lambda = jnp.log(p_lam) - jnp.log1p(-p_lam)
    return {
        'x_prompt': nrm((BATCH, SEQ, D_MODEL), 1.0),
        'x_sample': nrm((DEC_BATCH, DEC_SEQ, D_MODEL), 1.0),
        'c': nrm((DEC_BATCH, D_MODEL), 1.0),
        'cache_k': nrm((DEC_BATCH, DEPTH, H_B, PAST_LEN, 2 * DK), 1.0),
        'cache_v': nrm((DEC_BATCH, DEPTH, H_B, PAST_LEN, DV), 1.0),
        'state_rwkv': nrm((DEC_BATCH, DEPTH, 2, H_A, HD_A, HD_A), 0.3),
        'state_lru': nrm((DEC_BATCH, DEPTH, 2, W_C), 0.5),
        'c_ctx': nrm((D_MODEL,), 1.0),
        'mod_w': nrm((DEPTH, D_MODEL, 6 * D_MODEL), 0.5 * D_MODEL ** -0.5),
        'mod_b': nrm((DEPTH, 6 * D_MODEL), 0.02),
        'norm1_g': gain((DEPTH, D_MODEL)),
        'norm2_g': gain((DEPTH, D_MODEL)),
        'w_in': nrm((DEPTH, D_MODEL, D_IN), D_MODEL ** -0.5),
        'w_out': nrm((DEPTH, W_A + W_B + W_C, D_MODEL), (W_A + W_B + W_C) ** -0.5),
        'rwkv_w0': nrm((DEPTH, 2, W_A), 0.5),
        'rwkv_wA': nrm((DEPTH, 2, D_MODEL, RWKV_DECAY_LORA), D_MODEL ** -0.5),
        'rwkv_wB': nrm((DEPTH, 2, RWKV_DECAY_LORA, W_A), 0.5 * RWKV_DECAY_LORA ** -0.5),
        'rwkv_a0': nrm((DEPTH, 2, W_A), 0.5),
        'rwkv_aA': nrm((DEPTH, 2, D_MODEL, RWKV_ICLR_LORA), D_MODEL ** -0.5),
        'rwkv_aB': nrm((DEPTH, 2, RWKV_ICLR_LORA, W_A), 0.5 * RWKV_ICLR_LORA ** -0.5),
        'rwkv_gA': nrm((DEPTH, D_MODEL, RWKV_GATE_LORA), D_MODEL ** -0.5),
        'rwkv_gB': nrm((DEPTH, RWKV_GATE_LORA, W_A), RWKV_GATE_LORA ** -0.5),
        'rwkv_kk': 0.85 + nrm((DEPTH, W_A), 0.05),
        'rwkv_ka': 1.0 + nrm((DEPTH, W_A), 0.05),
        'rwkv_rk': nrm((DEPTH, H_A, HD_A), 0.1),
        'rwkv_ln_g': gain((DEPTH, W_A)),
        'diff_lq1': nrm((DEPTH, DK), 0.1),
        'diff_lk1': nrm((DEPTH, DK), 0.1),
        'diff_lq2': nrm((DEPTH, DK), 0.1),
        'diff_lk2': nrm((DEPTH, DK), 0.1),
        'diff_subln_g': gain((DEPTH, DV)),
        'lru_conv_w': nrm((DEPTH, CONV_W, W_C), 0.5),
        'lru_conv_b': nrm((DEPTH, W_C), 0.02),
        'lru_wa': nrm((DEPTH, 2, LRU_BLOCKS, LRU_BW, LRU_BW), LRU_BW ** -0.5),
        'lru_ba': nrm((DEPTH, 2, W_C), 0.02),
        'lru_wx': nrm((DEPTH, 2, LRU_BLOCKS, LRU_BW, LRU_BW), LRU_BW ** -0.5),
        'lru_bx': nrm((DEPTH, 2, W_C), 0.02),
        'lru_lambda': lru_lambda,
        'peer_wq': nrm((DEPTH, D_MODEL, PEER_HEADS * PEER_DQ), D_MODEL ** -0.5),
        'peer_keys': nrm((DEPTH, PEER_HEADS, 2, PEER_N_KEYS, PEER_DQ // 2), (PEER_DQ // 2) ** -0.5),
        'peer_u': nrm((DEPTH, N_EXPERTS, D_MODEL), D_MODEL ** -0.5),
        'peer_v': nrm((DEPTH, N_EXPERTS, D_MODEL), 0.25),
        'final_norm_g': gain((D_MODEL,)),
    }


def reference(x_prompt, x_sample, c, cache_k, cache_v, state_rwkv, state_lru, c_ctx,
              mod_w, mod_b, norm1_g, norm2_g, w_in, w_out,
              rwkv_w0, rwkv_wA, rwkv_wB, rwkv_a0, rwkv_aA, rwkv_aB, rwkv_gA, rwkv_gB,
              rwkv_kk, rwkv_ka, rwkv_rk, rwkv_ln_g,
              diff_lq1, diff_lk1, diff_lq2, diff_lk2, diff_subln_g,
              lru_conv_w, lru_conv_b, lru_wa, lru_ba, lru_wx, lru_bx, lru_lambda,
              peer_wq, peer_keys, peer_u, peer_v, final_norm_g):
    Bp = x_prompt.shape[0]
    Ts = x_sample.shape[1]
    rope = axial_rope_tables(Ts)
    zero_rwkv = jnp.zeros((Bp, 2, H_A, HD_A, HD_A), x_prompt.dtype)
    zero_lru = jnp.zeros((Bp, 2, W_C), x_prompt.dtype)
    xp = x_prompt
    xs = x_sample
    ks_list, vs_list, sr_list, sl_list = [], [], [], []
    for l in range(DEPTH):
        lp = {
            'w_in': w_in[l], 'w_out': w_out[l], 'norm1_g': norm1_g[l], 'norm2_g': norm2_g[l],
            'rwkv_w0': rwkv_w0[l], 'rwkv_wA': rwkv_wA[l], 'rwkv_wB': rwkv_wB[l],
            'rwkv_a0': rwkv_a0[l], 'rwkv_aA': rwkv_aA[l], 'rwkv_aB': rwkv_aB[l],
            'rwkv_gA': rwkv_gA[l], 'rwkv_gB': rwkv_gB[l], 'rwkv_kk': rwkv_kk[l], 'rwkv_ka': rwkv_ka[l],
            'rwkv_rk': rwkv_rk[l], 'rwkv_ln_g': rwkv_ln_g[l],
            'diff_lq1': diff_lq1[l], 'diff_lk1': diff_lk1[l], 'diff_lq2': diff_lq2[l], 'diff_lk2': diff_lk2[l],
            'diff_subln_g': diff_subln_g[l],
            'lru_conv_w': lru_conv_w[l], 'lru_conv_b': lru_conv_b[l], 'lru_wa': lru_wa[l], 'lru_ba': lru_ba[l],
            'lru_wx': lru_wx[l], 'lru_bx': lru_bx[l], 'lru_lambda': lru_lambda[l],
            'peer_wq': peer_wq[l], 'peer_keys': peer_keys[l], 'peer_u': peer_u[l], 'peer_v': peer_v[l],
        }
        lam_init = 0.8 - 0.6 * math.exp(-0.3 * l)
        mod_p = adaln_params(c_ctx[None, :], mod_w[l], mod_b[l])
        mod_s = adaln_params(c, mod_w[l], mod_b[l])
        xp, k_l, v_l, sr_l, sl_l = trunk_layer(xp, mod_p, lp, lam_init, None, None, None, zero_rwkv, zero_lru)
        xs, _, _, _, _ = trunk_layer(xs, mod_s, lp, lam_init, rope, cache_k[:, l], cache_v[:, l],
                                     state_rwkv[:, l], state_lru[:, l])
        ks_list.append(k_l)
        vs_list.append(v_l)
        sr_list.append(sr_l)
        sl_list.append(sl_l)
    y_prompt = rmsnorm(xp, final_norm_g)
    y_sample = rmsnorm(xs, final_norm_g)
    new_cache_k = jnp.stack(ks_list, axis=1)
    new_cache_v = jnp.stack(vs_list, axis=1)
    new_state_rwkv = jnp.stack(sr_list, axis=1)
    new_state_lru = jnp.stack(sl_list, axis=1)
    return (y_prompt, y_sample, new_cache_k, new_cache_v, new_state_rwkv, new_state_lru)
```

```python
import functools
import math

import jax
import jax.numpy as jnp
from jax import lax
from jax.experimental import pallas as pl
from jax.experimental.pallas import tpu as pltpu

F32 = jnp.float32
BF16 = jnp.bfloat16
HI = lax.Precision.HIGHEST
SDS = jax.ShapeDtypeStruct

D_MODEL = 1024
BATCH = 32
SEQ = 256
DEPTH = 2
DEC_BATCH = 8
DEC_SEQ = 1024
PAST_LEN = 512
GRID_W = 64
HEAD_DIM = 64
W_A = D_MODEL // 4
W_B = D_MODEL // 2
W_C = D_MODEL - W_A - W_B
H_A = W_A // HEAD_DIM
H_B = W_B // HEAD_DIM
DK = HEAD_DIM // 2
ROPE_PAIRS = DK // 4
ROPE_BASE = 10000.0
DECAY_SCALE = 0.606531
LORA_W = 64
LORA_G = 128
CONV_W = 4
CONV_LEFT = 2
LRU_C = 8.0
LRU_BLOCKS = W_C // HEAD_DIM
PEER_HEADS = 8
PEER_N_KEYS = 128
PEER_TOPK = 16
PEER_DQ = 256
N_EXPERTS = PEER_N_KEYS ** 2
D_IN = 3 * W_A + 3 * W_B + 2 * W_C
D_CAT = D_IN + 4 * LORA_W + LORA_G
RMS_EPS = 1e-6
GN_EPS = 64e-5
SUBLN_EPS = 1e-5
INV_SQRT2 = 0.7071067811865476

N_CTX = BATCH * SEQ
N_LAT = DEC_BATCH * DEC_SEQ
N_TOK = N_CTX + N_LAT

VMEM_LIMIT_BYTES = 56 * 1024 * 1024
TM = 256
TQ = 256
TC_SCAN = 16
TN_PEER = 512
ROWS_PEER = 8


def _cparams(*sem):
    return pltpu.CompilerParams(dimension_semantics=sem, vmem_limit_bytes=VMEM_LIMIT_BYTES)


def _mod_index(block_tokens):
    nbc = N_CTX // block_tokens
    per_seq = DEC_SEQ // block_tokens
    return lambda i: (jnp.where(i < nbc, 0, 1 + (i - nbc) // per_seq), 0, 0)


def _gelu(x):
    return 0.5 * x * (1.0 + lax.erf(x * INV_SQRT2))


def _dot_nt(a, b, precision=None):
    return lax.dot_general(a, b, (((1,), (1,)), ((), ())), precision=precision,
                           preferred_element_type=F32)


def _adaln_kernel(c_ref, w_ref, b_ref, o_ref):
    c = c_ref[...]
    s = c * jax.nn.sigmoid(c)
    o_ref[...] = jnp.dot(s, w_ref[...], precision=HI, preferred_element_type=F32) + b_ref[...]


def _adaln(cond, w, b):
    rows, n = cond.shape[0], w.shape[1]
    tn = 1536
    return pl.pallas_call(
        _adaln_kernel,
        grid=(n // tn,),
        in_specs=[pl.BlockSpec((rows, D_MODEL), lambda j: (0, 0)),
                  pl.BlockSpec((D_MODEL, tn), lambda j: (0, j)),
                  pl.BlockSpec((1, tn), lambda j: (0, j))],
        out_specs=pl.BlockSpec((rows, tn), lambda j: (0, j)),
        out_shape=SDS((rows, n), F32),
        compiler_params=_cparams("arbitrary"),
        name="adaln",
    )(cond, w, b[None])


def _proj_kernel(x_ref, mod_ref, g1_ref, wcat_ref, wb_ref, ab_ref, gb_ref, w0_ref, a0_ref,
                 kkw_ref, ka_ref, rk_ref, bd_ref, cos_ref, sa_ref, sb_ref,
                 r_o, v_o, nkk_o, w0_o, w1_o, b0_o, b1_o, kd0_o, kd1_o, bonus_o, g_o,
                 q_o, k_o, vv_o, xc_o, gc_o):
    x = x_ref[...]
    xn = x * lax.rsqrt(jnp.mean(x * x, axis=-1, keepdims=True) + RMS_EPS) * g1_ref[...]
    h = xn * (1.0 + mod_ref[0, 1:2, :]) + mod_ref[0, 0:1, :]
    p = jnp.dot(h.astype(BF16), wcat_ref[...], preferred_element_type=F32)

    r = p[:, 0:W_A]
    k = p[:, W_A:2 * W_A]
    v = p[:, 2 * W_A:3 * W_A]
    o = 3 * W_A
    q_b = p[:, o:o + W_B]
    k_b = p[:, o + W_B:o + 2 * W_B]
    v_b = p[:, o + 2 * W_B:o + 3 * W_B]
    o += 3 * W_B
    xc_o[...] = p[:, o:o + W_C]
    gc_o[...] = p[:, o + W_C:o + 2 * W_C]
    o = D_IN
    lw = jnp.tanh(p[:, o:o + 2 * LORA_W])
    la = p[:, o + 2 * LORA_W:o + 4 * LORA_W]
    lg = jax.nn.sigmoid(p[:, o + 4 * LORA_W:o + 4 * LORA_W + LORA_G])

    wpre = jnp.dot(lw.astype(BF16), wb_ref[...], preferred_element_type=F32) + w0_ref[...]
    apre = jnp.dot(la.astype(BF16), ab_ref[...], preferred_element_type=F32) + a0_ref[...]
    wdec = jnp.exp(-DECAY_SCALE * jax.nn.sigmoid(wpre))
    aicl = jax.nn.sigmoid(apre)
    g_o[...] = jnp.dot(lg.astype(BF16), gb_ref[...], preferred_element_type=F32)

    bd = bd_ref[...]
    kq = k * kkw_ref[...]
    ss = jnp.dot(kq * kq, bd, precision=HI, preferred_element_type=F32)
    kk = kq * lax.rsqrt(jnp.maximum(ss, 1e-12))
    nkk_o[...] = -kk
    r_o[...] = r
    v_o[...] = v
    ka = ka_ref[...]
    for d, (w_o, b_o, kd_o) in enumerate(((w0_o, b0_o, kd0_o), (w1_o, b1_o, kd1_o))):
        a_d = aicl[:, d * W_A:(d + 1) * W_A]
        w_o[...] = wdec[:, d * W_A:(d + 1) * W_A]
        b_o[...] = kk * a_d
        kd_o[...] = k * (1.0 + (a_d - 1.0) * ka)
    bonus_o[...] = jnp.dot(r * k * rk_ref[...], bd, precision=HI, preferred_element_type=F32) * v

    cos, sa, sb = cos_ref[...], sa_ref[...], sb_ref[...]

    def rope(z):
        return z * cos + pltpu.roll(z, W_B - 1, 1) * sa + pltpu.roll(z, 1, 1) * sb

    q_r = rope(q_b)
    k_r = rope(k_b)
    for hh in range(H_B):
        sl = slice(hh * HEAD_DIM, (hh + 1) * HEAD_DIM)
        q_o[hh] = q_r[:, sl]
        k_o[hh] = k_r[:, sl]
        vv_o[hh] = v_b[:, sl]


def _proj(x, mod, g1, wcat, wb, ab, gb, w0, a0, kkw, ka, rk, bd, cos_t, sa_t, sb_t):
    nb = N_TOK // TM
    nbc = N_CTX // TM
    per_seq = DEC_SEQ // TM
    ident_blk = per_seq

    def rope_idx(i):
        return (jnp.where(i < nbc, ident_blk, (i - nbc) % per_seq), 0)

    def whole(a):
        return pl.BlockSpec(a.shape, lambda i: (0,) * a.ndim)

    tok = lambda w: pl.BlockSpec((TM, w), lambda i: (i, 0))
    heads = pl.BlockSpec((H_B, TM, HEAD_DIM), lambda i: (0, i, 0))
    in_specs = [tok(D_MODEL), pl.BlockSpec((1, 6, D_MODEL), _mod_index(TM)), whole(g1), whole(wcat),
                whole(wb), whole(ab), whole(gb), whole(w0), whole(a0), whole(kkw), whole(ka),
                whole(rk), whole(bd),
                pl.BlockSpec((TM, W_B), rope_idx), pl.BlockSpec((TM, W_B), rope_idx),
                pl.BlockSpec((TM, W_B), rope_idx)]
    out_specs = [tok(W_A)] * 11 + [heads] * 3 + [tok(W_C)] * 2
    out_shape = ([SDS((N_TOK, W_A), F32)] * 11 + [SDS((H_B, N_TOK, HEAD_DIM), F32)] * 3
                 + [SDS((N_TOK, W_C), F32)] * 2)
    return pl.pallas_call(
        _proj_kernel, grid=(nb,), in_specs=in_specs, out_specs=out_specs, out_shape=out_shape,
        compiler_params=_cparams("arbitrary"), name="proj",
    )(x, mod, g1, wcat, wb, ab, gb, w0, a0, kkw, ka, rk, bd, cos_t, sa_t, sb_t)


def _rwkv_kernel(nkf, rf, wf, bf, kf, vf, nkb, rb, wb, bb, kb, vb, s0_ref,
                 of_ref, ob_ref, sl_ref, s_scr, *, tc):
    i = pl.program_id(0)

    @pl.when(i == 0)
    def _():
        s_scr[...] = s0_ref[...]

    nacc = 4

    def tree(parts):
        return (parts[0] + parts[1]) + (parts[2] + parts[3])

    def one_dir(d, t, nk, r, w, b, k, v, o):
        parts = [None] * nacc
        for j in range(HEAD_DIM):
            term = s_scr[d, j] * nk[t, pl.ds(j, 1), :]
            parts[j % nacc] = term if parts[j % nacc] is None else parts[j % nacc] + term
        sa = tree(parts)
        vt = v[t]
        parts = [None] * nacc
        for j in range(HEAD_DIM):
            sj = (s_scr[d, j] * w[t, pl.ds(j, 1), :] + sa * b[t, pl.ds(j, 1), :]
                  + vt * k[t, pl.ds(j, 1), :])
            s_scr[d, j] = sj
            term = sj * r[t, pl.ds(j, 1), :]
            parts[j % nacc] = term if parts[j % nacc] is None else parts[j % nacc] + term
        o[t] = tree(parts)

    def step(tt, carry):
        one_dir(0, tt, nkf, rf, wf, bf, kf, vf, of_ref)
        one_dir(1, tc - 1 - tt, nkb, rb, wb, bb, kb, vb, ob_ref)
        return carry

    lax.fori_loop(0, tc, step, 0)

    @pl.when(i == pl.num_programs(0) - 1)
    def _():
        sl_ref[...] = s_scr[...]


def _rwkv_scan(nkk, r, w0, b0, kd0, w1, b1, kd1, v, s0):
    t_len = nkk.shape[0]
    i_rows = v.shape[1]
    tc = TC_SCAN
    nt = t_len // tc
    fwd = lambda rows: pl.BlockSpec((tc, rows, 128), lambda i: (i, 0, 0))
    bwd = lambda rows: pl.BlockSpec((tc, rows, 128), lambda i: (nt - 1 - i, 0, 0))
    st = pl.BlockSpec((2, HEAD_DIM, i_rows, 128), lambda i: (0, 0, 0, 0))
    return pl.pallas_call(
        functools.partial(_rwkv_kernel, tc=tc),
        grid=(nt,),
        in_specs=[fwd(HEAD_DIM)] * 5 + [fwd(i_rows)] + [bwd(HEAD_DIM)] * 5 + [bwd(i_rows)] + [st],
        out_specs=[fwd(i_rows), bwd(i_rows), st],
        out_shape=[SDS((t_len, i_rows, 128), F32), SDS((t_len, i_rows, 128), F32),
                   SDS((2, HEAD_DIM, i_rows, 128), F32)],
        scratch_shapes=[pltpu.VMEM((2, HEAD_DIM, i_rows, 128), F32)],
        compiler_params=_cparams("arbitrary"), name="rwkv_scan",
    )(nkk, r, w0, b0, kd0, v, nkk, r, w1, b1, kd1, v, s0)


def _to_scan_keys(a, nb, t_len, i_hi):
    a = a.reshape(nb, t_len, H_A, HEAD_DIM).transpose(1, 3, 0, 2).reshape(t_len, HEAD_DIM, nb * H_A)
    return jnp.tile(a, (1, 1, i_hi))


def _to_scan_vals(a, nb, t_len, i_hi):
    i_lo = HEAD_DIM // i_hi
    a = a.reshape(nb, t_len, H_A, i_hi, i_lo).transpose(1, 4, 3, 0, 2)
    return a.reshape(t_len, i_lo, i_hi * nb * H_A)


def _from_scan_vals(o, nb, t_len, i_hi):
    i_lo = HEAD_DIM // i_hi
    o = o.reshape(t_len, i_lo, i_hi, nb, H_A).transpose(3, 0, 4, 2, 1)
    return o.reshape(nb * t_len, W_A)


def _state_to_scan(s, nb, i_hi):
    i_lo = HEAD_DIM // i_hi
    s = s.reshape(nb, 2, H_A, i_hi, i_lo, HEAD_DIM).transpose(1, 5, 4, 3, 0, 2)
    return s.reshape(2, HEAD_DIM, i_lo, i_hi * nb * H_A)


def _state_from_scan(s, nb, i_hi):
    i_lo = HEAD_DIM // i_hi
    s = s.reshape(2, HEAD_DIM, i_lo, i_hi, nb, H_A).transpose(4, 0, 5, 3, 2, 1)
    return s.reshape(nb, 2, H_A, HEAD_DIM, HEAD_DIM)


def _attn_kernel(*refs, has_ctx, out_scale):
    if has_ctx:
        lam_ref, q_ref, k_ref, v_ref, ck_ref, cv_ref, g_ref, o_ref = refs
    else:
        lam_ref, q_ref, k_ref, v_ref, g_ref, o_ref = refs
    lam = lam_ref[0]
    q = q_ref[0]
    lane = lax.broadcasted_iota(jnp.int32, q.shape, 1)
    scale = DK ** -0.5
    qs = [jnp.where(lane < DK, q, 0.0).astype(BF16), jnp.where(lane >= DK, q, 0.0).astype(BF16)]
    kn = k_ref[0].astype(BF16)
    vn = v_ref[0].astype(BF16)
    if has_ctx:
        kc = ck_ref[0, 0, 0].astype(BF16)
        vc = cv_ref[0, 0, 0].astype(BF16)
    p_new = None
    p_ctx = None
    for c in range(2):
        s_n = _dot_nt(qs[c], kn) * scale
        m = jnp.max(s_n, axis=-1, keepdims=True)
        if has_ctx:
            s_c = _dot_nt(qs[c], kc) * scale
            m = jnp.maximum(m, jnp.max(s_c, axis=-1, keepdims=True))
        e_n = jnp.exp(s_n - m)
        z = jnp.sum(e_n, axis=-1, keepdims=True)
        if has_ctx:
            e_c = jnp.exp(s_c - m)
            z = z + jnp.sum(e_c, axis=-1, keepdims=True)
        coef = (1.0 / z) if c == 0 else (-lam / z)
        p_new = e_n * coef if p_new is None else p_new + e_n * coef
        if has_ctx:
            p_ctx = e_c * coef if p_ctx is None else p_ctx + e_c * coef
    o = jnp.dot(p_new.astype(BF16), vn, preferred_element_type=F32)
    if has_ctx:
        o = o + jnp.dot(p_ctx.astype(BF16), vc, preferred_element_type=F32)
    o = o * lax.rsqrt(jnp.mean(o * o, axis=-1, keepdims=True) + SUBLN_EPS) * g_ref[...]
    o_ref[0] = o * out_scale


def _attention(lam, q, k, v, g, lam_init, *, nb, t_len, row_off, layer=None, ck=None, cv=None):
    has_ctx = ck is not None
    nq = t_len // TQ
    qoff = row_off // TQ
    koff = row_off // t_len
    in_specs = [pl.BlockSpec(memory_space=pltpu.SMEM),
                pl.BlockSpec((1, TQ, HEAD_DIM), lambda b, h, i: (h, qoff + b * nq + i, 0)),
                pl.BlockSpec((1, t_len, HEAD_DIM), lambda b, h, i: (h, koff + b, 0)),
                pl.BlockSpec((1, t_len, HEAD_DIM), lambda b, h, i: (h, koff + b, 0))]
    args = [lam, q, k, v]
    if has_ctx:
        cspec = pl.BlockSpec((1, 1, 1, PAST_LEN, HEAD_DIM), lambda b, h, i: (b, layer, h, 0, 0))
        in_specs += [cspec, cspec]
        args += [ck, cv]
    in_specs.append(pl.BlockSpec((1, HEAD_DIM), lambda b, h, i: (0, 0)))
    args.append(g)
    return pl.pallas_call(
        functools.partial(_attn_kernel, has_ctx=has_ctx, out_scale=1.0 - lam_init),
        grid=(nb, H_B, nq),
        in_specs=in_specs,
        out_specs=pl.BlockSpec((1, TQ, HEAD_DIM), lambda b, h, i: (h, b * nq + i, 0)),
        out_shape=SDS((H_B, nb * t_len, HEAD_DIM), F32),
        compiler_params=_cparams("arbitrary", "arbitrary", "arbitrary"),
        name="attn_ctx" if has_ctx else "attn",
    )(*args)


def _lru_kernel(xc_ref, gc_ref, cw_ref, cb_ref, wa_ref, ba_ref, wx_ref, bx_ref, lam_ref, s0_ref,
                y_ref, hl_ref, *, t_len):
    x = xc_ref[...]
    row = lax.broadcasted_iota(jnp.int32, x.shape, 0)

    def shifted(a, off, fill):
        if off == 0:
            return a
        if off < 0:
            return jnp.where(row >= -off, pltpu.roll(a, -off, 0), fill)
        return jnp.where(row < t_len - off, pltpu.roll(a, t_len - off, 0), fill)

    xconv = cb_ref[...]
    for j in range(CONV_W):
        xconv = xconv + shifted(x, j - CONV_LEFT, 0.0) * cw_ref[j:j + 1, :]
    hsum = None
    for d in range(2):
        gr = jax.nn.sigmoid(jnp.dot(xconv, wa_ref[d], precision=HI, preferred_element_type=F32)
                            + ba_ref[d:d + 1, :])
        gi = jax.nn.sigmoid(jnp.dot(xconv, wx_ref[d], precision=HI, preferred_element_type=F32)
                            + bx_ref[d:d + 1, :])
        nl = -lam_ref[d:d + 1, :]
        softplus = jnp.maximum(nl, 0.0) + jnp.log1p(jnp.exp(-jnp.abs(nl)))
        log_a = -LRU_C * gr * softplus
        a = jnp.exp(log_a)
        b = jnp.sqrt(1.0 - jnp.exp(2.0 * log_a)) * (gi * xconv)
        dist = 1
        while dist < t_len:
            off = -dist if d == 0 else dist
            a_sh = shifted(a, off, 1.0)
            b_sh = shifted(b, off, 0.0)
            b = a * b_sh + b
            a = a * a_sh
            dist *= 2
        hs = b + a * s0_ref[0, d:d + 1, :]
        last = t_len - 1 if d == 0 else 0
        hl_ref[0, d:d + 1, :] = hs[last:last + 1, :]
        hsum = hs if hsum is None else hsum + hs
    y_ref[...] = hsum * _gelu(gc_ref[...])


def _lru(xc, gc, cw, cb, wa, ba, wx, bx, lam, s0, *, nb, t_len, row_off):
    off = row_off // t_len

    def whole(a):
        return pl.BlockSpec(a.shape, lambda b: (0,) * a.ndim)

    tok = pl.BlockSpec((t_len, W_C), lambda b: (off + b, 0))
    st = pl.BlockSpec((1, 2, W_C), lambda b: (b, 0, 0))
    return pl.pallas_call(
        functools.partial(_lru_kernel, t_len=t_len),
        grid=(nb,),
        in_specs=[tok, tok, whole(cw), whole(cb), whole(wa), whole(ba), whole(wx), whole(bx),
                  whole(lam), st],
        out_specs=[pl.BlockSpec((t_len, W_C), lambda b: (b, 0)), st],
        out_shape=[SDS((nb * t_len, W_C), F32), SDS((nb, 2, W_C), F32)],
        compiler_params=_cparams("arbitrary"), name="lru",
    )(xc, gc, cw, cb, wa, ba, wx, bx, lam, s0)


def _mid_kernel(x_ref, mod_ref, of_ref, ob_ref, bonus_ref, g_ref, at_ref, y_ref, lng_ref, bd_ref,
                wout_ref, g2_ref, wq_ref, x1_o, h2_o, q_o):
    bd = bd_ref[...]
    o = of_ref[...] + ob_ref[...]
    inv_n = 1.0 / HEAD_DIM
    mu = jnp.dot(o, bd, precision=HI, preferred_element_type=F32) * inv_n
    xc = o - mu
    var = jnp.dot(xc * xc, bd, precision=HI, preferred_element_type=F32) * inv_n
    out_a = (xc * lax.rsqrt(var + GN_EPS) * lng_ref[...] + bonus_ref[...]) * g_ref[...]
    cat = jnp.concatenate([out_a] + [at_ref[hh] for hh in range(H_B)] + [y_ref[...]], axis=-1)
    ymix = jnp.dot(cat.astype(BF16), wout_ref[...], preferred_element_type=F32)
    x1 = x_ref[...] + mod_ref[0, 2:3, :] * ymix
    x1_o[...] = x1
    xn = x1 * lax.rsqrt(jnp.mean(x1 * x1, axis=-1, keepdims=True) + RMS_EPS) * g2_ref[...]
    h2 = xn * (1.0 + mod_ref[0, 4:5, :]) + mod_ref[0, 3:4, :]
    h2_o[...] = h2.astype(BF16)
    q_o[...] = jnp.dot(h2, wq_ref[...], precision=HI, preferred_element_type=F32)


def _mid(x, mod, o_f, o_b, bonus, g, att, y, lng, bd, wout, g2, wq):
    nb = N_TOK // TM

    def whole(a):
        return pl.BlockSpec(a.shape, lambda i: (0,) * a.ndim)

    tok = lambda w: pl.BlockSpec((TM, w), lambda i: (i, 0))
    return pl.pallas_call(
        _mid_kernel, grid=(nb,),
        in_specs=[tok(D_MODEL), pl.BlockSpec((1, 6, D_MODEL), _mod_index(TM)), tok(W_A), tok(W_A),
                  tok(W_A), tok(W_A), pl.BlockSpec((H_B, TM, HEAD_DIM), lambda i: (0, i, 0)),
                  tok(W_C), whole(lng), whole(bd), whole(wout), whole(g2), whole(wq)],
        out_specs=[tok(D_MODEL), tok(D_MODEL), tok(PEER_HEADS * PEER_DQ)],
        out_shape=[SDS((N_TOK, D_MODEL), F32), SDS((N_TOK, D_MODEL), BF16),
                   SDS((N_TOK, PEER_HEADS * PEER_DQ), F32)],
        compiler_params=_cparams("arbitrary"), name="mid",
    )(x, mod, o_f, o_b, bonus, g, att, y, lng, bd, wout, g2, wq)


def _topk_kernel(q_ref, keys_ref, r2_o, k1_o, e1_o, e2_o, t_scr):
    tn = q_ref.shape[0]
    kio = lax.broadcasted_iota(jnp.int32, (PEER_N_KEYS, tn), 0).astype(F32)
    aio = lax.broadcasted_iota(jnp.int32, (PEER_TOPK, tn), 0).astype(F32)
    neg_inf = -jnp.inf
    scores, ranks = [], []
    for c in range(2):
        half = PEER_DQ // 2
        st = _dot_nt(keys_ref[0, c], q_ref[:, c * half:(c + 1) * half], precision=HI)
        x = st
        rk = jnp.full((PEER_N_KEYS, tn), float(PEER_TOPK), F32)
        for kth in range(PEER_TOPK):
            m = jnp.max(x, axis=0, keepdims=True)
            first = jnp.min(jnp.where(x == m, kio, float(PEER_N_KEYS)), axis=0, keepdims=True)
            hit = kio == first
            rk = jnp.where(hit, float(kth), rk)
            x = jnp.where(hit, neg_inf, x)
            t_scr[c, kth:kth + 1, :] = m
        scores.append(st)
        ranks.append(rk)
    t1 = t_scr[0]
    t2 = t_scr[1]
    m0 = t1[0:1, :] + t2[0:1, :]
    f = t1 + t2[0:1, :]
    cnt = jnp.zeros((PEER_TOPK, tn), F32)
    z = jnp.zeros((1, tn), F32)
    for _ in range(PEER_TOPK):
        m = jnp.max(f, axis=0, keepdims=True)
        first = jnp.min(jnp.where(f == m, aio, float(PEER_TOPK)), axis=0, keepdims=True)
        hit = aio == first
        z = z + jnp.exp(m - m0)
        cnt = jnp.where(hit, cnt + 1.0, cnt)
        ch = jnp.sum(jnp.where(hit, cnt, 0.0), axis=0, keepdims=True)
        nt2 = jnp.sum(jnp.where(aio == ch, t2, 0.0), axis=0, keepdims=True)
        newf = jnp.where(ch >= float(PEER_TOPK), neg_inf, t1 + nt2)
        f = jnp.where(hit, newf, f)
    quota = jnp.zeros((PEER_N_KEYS, tn), F32)
    for a in range(PEER_TOPK):
        quota = jnp.where(ranks[0] == float(a), cnt[a:a + 1, :], quota)
    r2_o[0] = ranks[1]
    k1_o[0] = quota
    e1_o[0] = jnp.exp(scores[0] - t1[0:1, :]) * (1.0 / z)
    e2_o[0] = jnp.exp(scores[1] - t2[0:1, :])


def _peer_topk(q, keys):
    tn = TN_PEER
    out = SDS((PEER_HEADS, PEER_N_KEYS, N_TOK), F32)
    ospec = pl.BlockSpec((1, PEER_N_KEYS, tn), lambda i, h: (h, 0, i))
    return pl.pallas_call(
        _topk_kernel, grid=(N_TOK // tn, PEER_HEADS),
        in_specs=[pl.BlockSpec((tn, PEER_DQ), lambda i, h: (i, h)),
                  pl.BlockSpec((1, 2, PEER_N_KEYS, PEER_DQ // 2), lambda i, h: (h, 0, 0, 0))],
        out_specs=[ospec] * 4, out_shape=[out] * 4,
        scratch_shapes=[pltpu.VMEM((2, PEER_TOPK, tn), F32)],
        compiler_params=_cparams("arbitrary", "arbitrary"), name="peer_topk",
    )(q, keys)


def _dense_kernel(h2_ref, u_ref, vt_ref, r2_ref, e2_ref, k1_ref, e1_ref, x1_ref, mod_ref, fg_ref,
                  o_ref, acc, *, final):
    j = pl.program_id(1)

    @pl.when(j == 0)
    def _():
        acc[...] = jnp.zeros_like(acc)

    act = _gelu(_dot_nt(u_ref[...], h2_ref[...]))
    parts = []
    for r in range(ROWS_PEER):
        gate = None
        for hh in range(PEER_HEADS):
            sel = r2_ref[hh] < k1_ref[hh, r:r + 1, :]
            term = jnp.where(sel, e2_ref[hh] * e1_ref[hh, r:r + 1, :], 0.0)
            gate = term if gate is None else gate + term
        parts.append((gate * act[r * PEER_N_KEYS:(r + 1) * PEER_N_KEYS]).astype(BF16))
    p = jnp.concatenate(parts, axis=0)
    acc[...] += jnp.dot(vt_ref[...], p, preferred_element_type=F32)

    @pl.when(j == pl.num_programs(1) - 1)
    def _():
        x2 = x1_ref[...] + mod_ref[0, 5:6, :] * acc[...].T
        if final:
            x2 = x2 * lax.rsqrt(jnp.mean(x2 * x2, axis=-1, keepdims=True) + RMS_EPS) * fg_ref[...]
        o_ref[...] = x2


def _peer_dense(h2, u, vt, r2, e2, k1, e1, x1, mod, fg, *, final):
    tn = TN_PEER
    ne = ROWS_PEER * PEER_N_KEYS
    per_tok = pl.BlockSpec((PEER_HEADS, PEER_N_KEYS, tn), lambda i, j: (0, 0, i))
    per_row = pl.BlockSpec((PEER_HEADS, ROWS_PEER, tn), lambda i, j: (0, j, i))
    mod_idx = _mod_index(tn)
    return pl.pallas_call(
        functools.partial(_dense_kernel, final=final),
        grid=(N_TOK // tn, N_EXPERTS // ne),
        in_specs=[pl.BlockSpec((tn, D_MODEL), lambda i, j: (i, 0)),
                  pl.BlockSpec((ne, D_MODEL), lambda i, j: (j, 0)),
                  pl.BlockSpec((D_MODEL, ne), lambda i, j: (0, j)),
                  per_tok, per_tok, per_row, per_row,
                  pl.BlockSpec((tn, D_MODEL), lambda i, j: (i, 0)),
                  pl.BlockSpec((1, 6, D_MODEL), lambda i, j: mod_idx(i)),
                  pl.BlockSpec((1, D_MODEL), lambda i, j: (0, 0))],
        out_specs=pl.BlockSpec((tn, D_MODEL), lambda i, j: (i, 0)),
        out_shape=SDS((N_TOK, D_MODEL), F32),
        scratch_shapes=[pltpu.VMEM((D_MODEL, tn), F32)],
        compiler_params=_cparams("arbitrary", "arbitrary"), name="peer_dense",
    )(h2, u, vt, r2, e2, k1, e1, x1, mod, fg)


def _block_diag(blocks):
    g, n, m = blocks.shape
    eye = jnp.eye(g, dtype=blocks.dtype)
    return (eye[:, None, :, None] * blocks[:, :, None, :]).reshape(g * n, g * m)


def _rope_tables():
    rows = DEC_SEQ // GRID_W
    row = jnp.repeat(jnp.arange(rows), GRID_W).astype(F32)
    col = jnp.tile(jnp.arange(GRID_W), rows).astype(F32)
    inv = ROPE_BASE ** (-jnp.arange(ROPE_PAIRS, dtype=F32) / ROPE_PAIRS)
    ang = jnp.concatenate([row[:, None] * inv, col[:, None] * inv], axis=-1)
    lane = jnp.arange(W_B)
    e = (lane % HEAD_DIM) % DK
    pair, parity = e // 2, e % 2
    cos = jnp.cos(ang)[:, pair]
    sin = jnp.sin(ang)[:, pair]
    sa = jnp.where(parity == 0, -sin, 0.0)
    sb = jnp.where(parity == 1, sin, 0.0)
    ident = jnp.ones((TM, W_B), F32)
    zero = jnp.zeros((TM, W_B), F32)
    return (jnp.concatenate([cos, ident]), jnp.concatenate([sa, zero]), jnp.concatenate([sb, zero]))


def kernel(x_prompt, x_sample, c, cache_k, cache_v, state_rwkv, state_lru, c_ctx, mod_w, mod_b, norm1_g, norm2_g, w_in, w_out, rwkv_w0, rwkv_wA, rwkv_wB, rwkv_a0, rwkv_aA, rwkv_aB, rwkv_gA, rwkv_gB, rwkv_kk, rwkv_ka, rwkv_rk, rwkv_ln_g, diff_lq1, diff_lk1, diff_lq2, diff_lk2, diff_subln_g, lru_conv_w, lru_conv_b, lru_wa, lru_ba, lru_wx, lru_bx, lru_lambda, peer_wq, peer_keys, peer_u, peer_v, final_norm_g):
    x = jnp.concatenate([x_prompt.reshape(N_CTX, D_MODEL), x_sample.reshape(N_LAT, D_MODEL)])
    cond = jnp.concatenate([c_ctx[None], c, jnp.zeros((16 - 1 - DEC_BATCH, D_MODEL), F32)])
    cos_t, sa_t, sb_t = _rope_tables()
    head_id = jnp.arange(W_A) // HEAD_DIM
    bd = (head_id[:, None] == head_id[None, :]).astype(F32)
    zero_rwkv = jnp.zeros((2, HEAD_DIM, HEAD_DIM, BATCH * H_A), F32)
    zero_lru = jnp.zeros((BATCH, 2, W_C), F32)
    i_hi_lat = 128 // (DEC_BATCH * H_A)
    fg = final_norm_g[None]

    ks_list, vs_list, sr_list, sl_list = [], [], [], []
    for l in range(DEPTH):
        lam_init = 0.8 - 0.6 * math.exp(-0.3 * l)
        lam = (jnp.exp(jnp.sum(diff_lq1[l] * diff_lk1[l])) - jnp.exp(jnp.sum(diff_lq2[l] * diff_lk2[l]))
               + lam_init).reshape(1)
        mod = _adaln(cond, mod_w[l], mod_b[l])[:1 + DEC_BATCH].reshape(1 + DEC_BATCH, 6, D_MODEL)

        wcat = jnp.concatenate([w_in[l], rwkv_wA[l, 0], rwkv_wA[l, 1], rwkv_aA[l, 0], rwkv_aA[l, 1],
                                rwkv_gA[l]], axis=1).astype(BF16)
        (r, v, nkk, w0, w1, b0, b1, kd0, kd1, bonus, g, q_h, k_h, v_h, xc, gc) = _proj(
            x, mod, norm1_g[l][None], wcat,
            _block_diag(rwkv_wB[l]).astype(BF16), _block_diag(rwkv_aB[l]).astype(BF16),
            rwkv_gB[l].astype(BF16), rwkv_w0[l].reshape(1, 2 * W_A), rwkv_a0[l].reshape(1, 2 * W_A),
            rwkv_kk[l][None], rwkv_ka[l][None], rwkv_rk[l].reshape(1, W_A), bd, cos_t, sa_t, sb_t)

        o_f, o_b = [], []
        for (lo, nb, t_len, i_hi, s0) in (
                (0, BATCH, SEQ, 1, zero_rwkv),
                (N_CTX, DEC_BATCH, DEC_SEQ, i_hi_lat, _state_to_scan(state_rwkv[:, l], DEC_BATCH, i_hi_lat))):
            sl = slice(lo, lo + nb * t_len)
            keys = [_to_scan_keys(a[sl], nb, t_len, i_hi) for a in (nkk, r, w0, b0, kd0, w1, b1, kd1)]
            of_s, ob_s, s_last = _rwkv_scan(*keys, _to_scan_vals(v[sl], nb, t_len, i_hi), s0)
            o_f.append(_from_scan_vals(of_s, nb, t_len, i_hi))
            o_b.append(_from_scan_vals(ob_s, nb, t_len, i_hi))
            if lo == 0:
                sr_list.append(_state_from_scan(s_last, nb, i_hi))
        o_f = jnp.concatenate(o_f)
        o_b = jnp.concatenate(o_b)

        sub_g = diff_subln_g[l][None]
        att_p = _attention(lam, q_h, k_h, v_h, sub_g, lam_init, nb=BATCH, t_len=SEQ, row_off=0)
        att_s = _attention(lam, q_h, k_h, v_h, sub_g, lam_init, nb=DEC_BATCH, t_len=DEC_SEQ,
                           row_off=N_CTX, layer=l, ck=cache_k, cv=cache_v)
        att = jnp.concatenate([att_p, att_s], axis=1)
        ks_list.append(k_h[:, :N_CTX].reshape(H_B, BATCH, SEQ, HEAD_DIM).transpose(1, 0, 2, 3))
        vs_list.append(v_h[:, :N_CTX].reshape(H_B, BATCH, SEQ, HEAD_DIM).transpose(1, 0, 2, 3))

        lru_args = (lru_conv_w[l], lru_conv_b[l][None], jax.vmap(_block_diag)(lru_wa[l]), lru_ba[l],
                    jax.vmap(_block_diag)(lru_wx[l]), lru_bx[l], lru_lambda[l])
        y_p, hl_p = _lru(xc, gc, *lru_args, zero_lru, nb=BATCH, t_len=SEQ, row_off=0)
        y_s, _ = _lru(xc, gc, *lru_args, state_lru[:, l], nb=DEC_BATCH, t_len=DEC_SEQ, row_off=N_CTX)
        sl_list.append(hl_p)
        y = jnp.concatenate([y_p, y_s])

        x1, h2, q = _mid(x, mod, o_f, o_b, bonus, g, att, y, rwkv_ln_g[l][None], bd,
                         w_out[l].astype(BF16), norm2_g[l][None], peer_wq[l])
        r2, k1, e1, e2 = _peer_topk(q, peer_keys[l])
        x = _peer_dense(h2, peer_u[l].astype(BF16), peer_v[l].T.astype(BF16), r2, e2, k1, e1, x1, mod, fg,
                        final=(l == DEPTH - 1))

    y_prompt = x[:N_CTX].reshape(BATCH, SEQ, D_MODEL)
    y_sample = x[N_CTX:].reshape(DEC_BATCH, DEC_SEQ, D_MODEL)
    return (y_prompt, y_sample, jnp.stack(ks_list, axis=1), jnp.stack(vs_list, axis=1),
            jnp.stack(sr_list, axis=1), jnp.stack(sl_list, axis=1))
```

```python
import functools
import math

import jax
import jax.numpy as jnp
from jax import lax
from jax.experimental import pallas as pl
from jax.experimental.pallas import tpu as pltpu

F32 = jnp.float32
BF16 = jnp.bfloat16
HI = lax.Precision.HIGHEST
SDS = jax.ShapeDtypeStruct

D_MODEL = 1024
BATCH = 32
SEQ = 256
DEPTH = 2
DEC_BATCH = 8
DEC_SEQ = 1024
PAST_LEN = 512
GRID_W = 64
HEAD_DIM = 64
W_A = D_MODEL // 4
W_B = D_MODEL // 2
W_C = D_MODEL - W_A - W_B
H_A = W_A // HEAD_DIM
H_B = W_B // HEAD_DIM
DK = HEAD_DIM // 2
ROPE_PAIRS = DK // 4
ROPE_BASE = 10000.0
DECAY_SCALE = 0.606531
LORA_W = 64
LORA_G = 128
CONV_W = 4
CONV_LEFT = 2
LRU_C = 8.0
LRU_BLOCKS = W_C // HEAD_DIM
PEER_HEADS = 8
PEER_N_KEYS = 128
PEER_TOPK = 16
PEER_DQ = 256
N_EXPERTS = PEER_N_KEYS ** 2
D_IN = 3 * W_A + 3 * W_B + 2 * W_C
D_CAT = D_IN + 4 * LORA_W + LORA_G
RMS_EPS = 1e-6
GN_EPS = 64e-5
SUBLN_EPS = 1e-5
INV_SQRT2 = 0.7071067811865476

N_CTX = BATCH * SEQ
N_LAT = DEC_BATCH * DEC_SEQ
N_TOK = N_CTX + N_LAT

VMEM_LIMIT_BYTES = 56 * 1024 * 1024
TM = 256
TQ = 256
TC_SCAN = 16
TN_PEER = 512
ROWS_PEER = 8
N_SCAN_KEYS = 8
BF16_SUBLANES = 16


def _cparams(*sem):
    return pltpu.CompilerParams(dimension_semantics=sem, vmem_limit_bytes=VMEM_LIMIT_BYTES)


def _mod_index(block_tokens):
    nbc = N_CTX // block_tokens
    per_seq = DEC_SEQ // block_tokens
    return lambda i: (jnp.where(i < nbc, 0, 1 + (i - nbc) // per_seq), 0, 0)


def _gelu(x):
    return 0.5 * x * (1.0 + lax.erf(x * INV_SQRT2))


def _dot_nt(a, b, precision=None):
    return lax.dot_general(a, b, (((1,), (1,)), ((), ())), precision=precision,
                           preferred_element_type=F32)


def _adaln_kernel(c_ref, w_ref, b_ref, o_ref):
    c = c_ref[...]
    s = c * jax.nn.sigmoid(c)
    o_ref[...] = jnp.dot(s, w_ref[...], precision=HI, preferred_element_type=F32) + b_ref[...]


def _adaln(cond, w, b):
    rows, n = cond.shape[0], w.shape[1]
    tn = 1536
    return pl.pallas_call(
        _adaln_kernel,
        grid=(n // tn,),
        in_specs=[pl.BlockSpec((rows, D_MODEL), lambda j: (0, 0)),
                  pl.BlockSpec((D_MODEL, tn), lambda j: (0, j)),
                  pl.BlockSpec((1, tn), lambda j: (0, j))],
        out_specs=pl.BlockSpec((rows, tn), lambda j: (0, j)),
        out_shape=SDS((rows, n), F32),
        compiler_params=_cparams("arbitrary"),
        name="adaln",
    )(cond, w, b[None])


def _proj_kernel(x_ref, mod_ref, g1_ref, wcat_ref, wb_ref, ab_ref, gb_ref, w0_ref, a0_ref,
                 kkw_ref, ka_ref, rk_ref, bd_ref, cos_ref, sa_ref, sb_ref,
                 keys_o, v_o, bonus_o, g_o, q_o, k_o, vv_o, xc_o, gc_o):
    x = x_ref[...]
    xn = x * lax.rsqrt(jnp.mean(x * x, axis=-1, keepdims=True) + RMS_EPS) * g1_ref[...]
    h = xn * (1.0 + mod_ref[0, 1:2, :]) + mod_ref[0, 0:1, :]
    p = jnp.dot(h.astype(BF16), wcat_ref[...], preferred_element_type=F32)

    r = p[:, 0:W_A]
    k = p[:, W_A:2 * W_A]
    v = p[:, 2 * W_A:3 * W_A]
    o = 3 * W_A
    q_b = p[:, o:o + W_B]
    k_b = p[:, o + W_B:o + 2 * W_B]
    v_b = p[:, o + 2 * W_B:o + 3 * W_B]
    o += 3 * W_B
    xc_o[...] = p[:, o:o + W_C]
    gc_o[...] = p[:, o + W_C:o + 2 * W_C]
    o = D_IN
    lw = jnp.tanh(p[:, o:o + 2 * LORA_W])
    la = p[:, o + 2 * LORA_W:o + 4 * LORA_W]
    lg = jax.nn.sigmoid(p[:, o + 4 * LORA_W:o + 4 * LORA_W + LORA_G])

    wpre = jnp.dot(lw.astype(BF16), wb_ref[...], preferred_element_type=F32) + w0_ref[...]
    apre = jnp.dot(la.astype(BF16), ab_ref[...], preferred_element_type=F32) + a0_ref[...]
    wdec = jnp.exp(-DECAY_SCALE * jax.nn.sigmoid(wpre))
    aicl = jax.nn.sigmoid(apre)
    g_o[...] = jnp.dot(lg.astype(BF16), gb_ref[...], preferred_element_type=F32)

    bd = bd_ref[...]
    kq = k * kkw_ref[...]
    ss = jnp.dot(kq * kq, bd, precision=HI, preferred_element_type=F32)
    kk = kq * lax.rsqrt(jnp.maximum(ss, 1e-12))
    keys_o[:, 0:W_A] = -kk
    keys_o[:, W_A:2 * W_A] = r
    v_o[...] = v
    ka = ka_ref[...]
    for d in range(2):
        a_d = aicl[:, d * W_A:(d + 1) * W_A]
        base = (2 + 3 * d) * W_A
        keys_o[:, base:base + W_A] = wdec[:, d * W_A:(d + 1) * W_A]
        keys_o[:, base + W_A:base + 2 * W_A] = kk * a_d
        keys_o[:, base + 2 * W_A:base + 3 * W_A] = k * (1.0 + (a_d - 1.0) * ka)
    bonus_o[...] = jnp.dot(r * k * rk_ref[...], bd, precision=HI, preferred_element_type=F32) * v

    cos, sa, sb = cos_ref[...], sa_ref[...], sb_ref[...]

    def rope(z):
        return z * cos + pltpu.roll(z, W_B - 1, 1) * sa + pltpu.roll(z, 1, 1) * sb

    q_r = rope(q_b)
    k_r = rope(k_b)
    for hh in range(H_B):
        sl = slice(hh * HEAD_DIM, (hh + 1) * HEAD_DIM)
        q_o[hh] = q_r[:, sl]
        k_o[hh] = k_r[:, sl]
        vv_o[hh] = v_b[:, sl]


def _proj(x, mod, g1, wcat, wb, ab, gb, w0, a0, kkw, ka, rk, bd, cos_t, sa_t, sb_t):
    nb = N_TOK // TM
    nbc = N_CTX // TM
    per_seq = DEC_SEQ // TM
    ident_blk = per_seq

    def rope_idx(i):
        return (jnp.where(i < nbc, ident_blk, (i - nbc) % per_seq), 0)

    def whole(a):
        return pl.BlockSpec(a.shape, lambda i: (0,) * a.ndim)

    tok = lambda w: pl.BlockSpec((TM, w), lambda i: (i, 0))
    heads = pl.BlockSpec((H_B, TM, HEAD_DIM), lambda i: (0, i, 0))
    in_specs = [tok(D_MODEL), pl.BlockSpec((1, 6, D_MODEL), _mod_index(TM)), whole(g1), whole(wcat),
                whole(wb), whole(ab), whole(gb), whole(w0), whole(a0), whole(kkw), whole(ka),
                whole(rk), whole(bd),
                pl.BlockSpec((TM, W_B), rope_idx), pl.BlockSpec((TM, W_B), rope_idx),
                pl.BlockSpec((TM, W_B), rope_idx)]
    out_specs = [tok(N_SCAN_KEYS * W_A)] + [tok(W_A)] * 3 + [heads] * 3 + [tok(W_C)] * 2
    out_shape = ([SDS((N_TOK, N_SCAN_KEYS * W_A), F32)] + [SDS((N_TOK, W_A), F32)] * 3
                 + [SDS((H_B, N_TOK, HEAD_DIM), F32)] * 3 + [SDS((N_TOK, W_C), F32)] * 2)
    return pl.pallas_call(
        _proj_kernel, grid=(nb,), in_specs=in_specs, out_specs=out_specs, out_shape=out_shape,
        compiler_params=_cparams("arbitrary"), name="proj",
    )(x, mod, g1, wcat, wb, ab, gb, w0, a0, kkw, ka, rk, bd, cos_t, sa_t, sb_t)


def _rwkv_kernel(keys_f, vf, keys_b, vb, s0_ref, of_ref, ob_ref, sl_ref, s_scr, *, tc):
    i = pl.program_id(0)

    @pl.when(i == 0)
    def _():
        s_scr[...] = s0_ref[...]

    nacc = 4

    def tree(parts):
        return (parts[0] + parts[1]) + (parts[2] + parts[3])

    def one_dir(d, t, keys, v, o):
        nk, r, w, b, k = 0, 1, 2 + 3 * d, 3 + 3 * d, 4 + 3 * d

        def row(a, j):
            return keys[t, a, pl.ds(j, 1), :]

        parts = [None] * nacc
        for j in range(HEAD_DIM):
            term = s_scr[d, j] * row(nk, j)
            parts[j % nacc] = term if parts[j % nacc] is None else parts[j % nacc] + term
        sa = tree(parts)
        vt = v[t]
        parts = [None] * nacc
        for j in range(HEAD_DIM):
            sj = s_scr[d, j] * row(w, j) + sa * row(b, j) + vt * row(k, j)
            s_scr[d, j] = sj
            term = sj * row(r, j)
            parts[j % nacc] = term if parts[j % nacc] is None else parts[j % nacc] + term
        o[t] = tree(parts)

    def step(tt, carry):
        one_dir(0, tt, keys_f, vf, of_ref)
        one_dir(1, tc - 1 - tt, keys_b, vb, ob_ref)
        return carry

    lax.fori_loop(0, tc, step, 0)

    @pl.when(i == pl.num_programs(0) - 1)
    def _():
        sl_ref[...] = s_scr[...]


def _rwkv_scan(keys, v, s0):
    t_len = keys.shape[0]
    i_rows = v.shape[1]
    tc = TC_SCAN
    nt = t_len // tc
    fwd = lambda rows: pl.BlockSpec((tc, rows, 128), lambda i: (i, 0, 0))
    bwd = lambda rows: pl.BlockSpec((tc, rows, 128), lambda i: (nt - 1 - i, 0, 0))
    kfwd = pl.BlockSpec((tc, N_SCAN_KEYS, HEAD_DIM, 128), lambda i: (i, 0, 0, 0))
    kbwd = pl.BlockSpec((tc, N_SCAN_KEYS, HEAD_DIM, 128), lambda i: (nt - 1 - i, 0, 0, 0))
    st = pl.BlockSpec((2, HEAD_DIM, i_rows, 128), lambda i: (0, 0, 0, 0))
    return pl.pallas_call(
        functools.partial(_rwkv_kernel, tc=tc),
        grid=(nt,),
        in_specs=[kfwd, fwd(i_rows), kbwd, bwd(i_rows), st],
        out_specs=[fwd(i_rows), bwd(i_rows), st],
        out_shape=[SDS((t_len, i_rows, 128), F32), SDS((t_len, i_rows, 128), F32),
                   SDS((2, HEAD_DIM, i_rows, 128), F32)],
        scratch_shapes=[pltpu.VMEM((2, HEAD_DIM, i_rows, 128), F32)],
        compiler_params=_cparams("arbitrary"), name="rwkv_scan",
    )(keys, v, keys, v, s0)


def _to_scan_keys(a, nb, t_len, i_hi):
    a = a.reshape(nb, t_len, N_SCAN_KEYS, H_A, HEAD_DIM).transpose(1, 2, 4, 0, 3)
    return jnp.tile(a.reshape(t_len, N_SCAN_KEYS, HEAD_DIM, nb * H_A), (1, 1, 1, i_hi))


def _to_scan_vals(a, nb, t_len, i_hi):
    i_lo = HEAD_DIM // i_hi
    a = a.reshape(nb, t_len, H_A, i_hi, i_lo).transpose(1, 4, 3, 0, 2)
    return a.reshape(t_len, i_lo, i_hi * nb * H_A)


def _from_scan_vals(o, nb, t_len, i_hi):
    i_lo = HEAD_DIM // i_hi
    o = o.reshape(t_len, i_lo, i_hi, nb, H_A).transpose(3, 0, 4, 2, 1)
    return o.reshape(nb * t_len, W_A)


def _state_to_scan(s, nb, i_hi):
    i_lo = HEAD_DIM // i_hi
    s = s.reshape(nb, 2, H_A, i_hi, i_lo, HEAD_DIM).transpose(1, 5, 4, 3, 0, 2)
    return s.reshape(2, HEAD_DIM, i_lo, i_hi * nb * H_A)


def _state_from_scan(s, nb, i_hi):
    i_lo = HEAD_DIM // i_hi
    s = s.reshape(2, HEAD_DIM, i_lo, i_hi, nb, H_A).transpose(4, 0, 5, 3, 2, 1)
    return s.reshape(nb, 2, H_A, HEAD_DIM, HEAD_DIM)


def _attn_kernel(*refs, has_ctx, out_scale):
    if has_ctx:
        lam_ref, q_ref, k_ref, v_ref, ck_ref, cv_ref, g_ref, o_ref = refs
    else:
        lam_ref, q_ref, k_ref, v_ref, g_ref, o_ref = refs
    lam = lam_ref[0]
    q = q_ref[0]
    lane = lax.broadcasted_iota(jnp.int32, q.shape, 1)
    scale = DK ** -0.5
    qs = [jnp.where(lane < DK, q, 0.0).astype(BF16), jnp.where(lane >= DK, q, 0.0).astype(BF16)]
    kn = k_ref[0].astype(BF16)
    vn = v_ref[0].astype(BF16)
    if has_ctx:
        kc = ck_ref[0, 0, 0].astype(BF16)
        vc = cv_ref[0, 0, 0].astype(BF16)
    p_new = None
    p_ctx = None
    for c in range(2):
        s_n = _dot_nt(qs[c], kn) * scale
        m = jnp.max(s_n, axis=-1, keepdims=True)
        if has_ctx:
            s_c = _dot_nt(qs[c], kc) * scale
            m = jnp.maximum(m, jnp.max(s_c, axis=-1, keepdims=True))
        e_n = jnp.exp(s_n - m)
        z = jnp.sum(e_n, axis=-1, keepdims=True)
        if has_ctx:
            e_c = jnp.exp(s_c - m)
            z = z + jnp.sum(e_c, axis=-1, keepdims=True)
        coef = (1.0 / z) if c == 0 else (-lam / z)
        p_new = e_n * coef if p_new is None else p_new + e_n * coef
        if has_ctx:
            p_ctx = e_c * coef if p_ctx is None else p_ctx + e_c * coef
    o = jnp.dot(p_new.astype(BF16), vn, preferred_element_type=F32)
    if has_ctx:
        o = o + jnp.dot(p_ctx.astype(BF16), vc, preferred_element_type=F32)
    o = o * lax.rsqrt(jnp.mean(o * o, axis=-1, keepdims=True) + SUBLN_EPS) * g_ref[...]
    o_ref[0] = o * out_scale


def _attention(lam, q, k, v, g, lam_init, *, nb, t_len, row_off, layer=None, ck=None, cv=None):
    has_ctx = ck is not None
    nq = t_len // TQ
    qoff = row_off // TQ
    koff = row_off // t_len
    in_specs = [pl.BlockSpec(memory_space=pltpu.SMEM),
                pl.BlockSpec((1, TQ, HEAD_DIM), lambda b, h, i: (h, qoff + b * nq + i, 0)),
                pl.BlockSpec((1, t_len, HEAD_DIM), lambda b, h, i: (h, koff + b, 0)),
                pl.BlockSpec((1, t_len, HEAD_DIM), lambda b, h, i: (h, koff + b, 0))]
    args = [lam, q, k, v]
    if has_ctx:
        cspec = pl.BlockSpec((1, 1, 1, PAST_LEN, HEAD_DIM), lambda b, h, i: (b, layer, h, 0, 0))
        in_specs += [cspec, cspec]
        args += [ck, cv]
    in_specs.append(pl.BlockSpec((1, HEAD_DIM), lambda b, h, i: (0, 0)))
    args.append(g)
    return pl.pallas_call(
        functools.partial(_attn_kernel, has_ctx=has_ctx, out_scale=1.0 - lam_init),
        grid=(nb, H_B, nq),
        in_specs=in_specs,
        out_specs=pl.BlockSpec((1, TQ, HEAD_DIM), lambda b, h, i: (h, b * nq + i, 0)),
        out_shape=SDS((H_B, nb * t_len, HEAD_DIM), F32),
        compiler_params=_cparams("arbitrary", "arbitrary", "arbitrary"),
        name="attn_ctx" if has_ctx else "attn",
    )(*args)


def _lru_kernel(xc_ref, gc_ref, cw_ref, cb_ref, wa_ref, ba_ref, wx_ref, bx_ref, lam_ref, s0_ref,
                y_ref, hl_ref, *, t_len):
    x = xc_ref[...]
    row = lax.broadcasted_iota(jnp.int32, x.shape, 0)

    def shifted(a, off, fill):
        if off == 0:
            return a
        if off < 0:
            return jnp.where(row >= -off, pltpu.roll(a, -off, 0), fill)
        return jnp.where(row < t_len - off, pltpu.roll(a, t_len - off, 0), fill)

    xconv = cb_ref[...]
    for j in range(CONV_W):
        xconv = xconv + shifted(x, j - CONV_LEFT, 0.0) * cw_ref[j:j + 1, :]
    hsum = None
    for d in range(2):
        gr = jax.nn.sigmoid(jnp.dot(xconv, wa_ref[d], precision=HI, preferred_element_type=F32)
                            + ba_ref[d:d + 1, :])
        gi = jax.nn.sigmoid(jnp.dot(xconv, wx_ref[d], precision=HI, preferred_element_type=F32)
                            + bx_ref[d:d + 1, :])
        nl = -lam_ref[d:d + 1, :]
        softplus = jnp.maximum(nl, 0.0) + jnp.log1p(jnp.exp(-jnp.abs(nl)))
        log_a = -LRU_C * gr * softplus
        a = jnp.exp(log_a)
        b = jnp.sqrt(1.0 - jnp.exp(2.0 * log_a)) * (gi * xconv)
        dist = 1
        while dist < t_len:
            off = -dist if d == 0 else dist
            a_sh = shifted(a, off, 1.0)
            b_sh = shifted(b, off, 0.0)
            b = a * b_sh + b
            a = a * a_sh
            dist *= 2
        hs = b + a * s0_ref[0, d:d + 1, :]
        last = t_len - 1 if d == 0 else 0
        hl_ref[0, d:d + 1, :] = hs[last:last + 1, :]
        hsum = hs if hsum is None else hsum + hs
    y_ref[...] = hsum * _gelu(gc_ref[...])


def _lru(xc, gc, cw, cb, wa, ba, wx, bx, lam, s0, *, nb, t_len, row_off):
    off = row_off // t_len

    def whole(a):
        return pl.BlockSpec(a.shape, lambda b: (0,) * a.ndim)

    tok = pl.BlockSpec((t_len, W_C), lambda b: (off + b, 0))
    st = pl.BlockSpec((1, 2, W_C), lambda b: (b, 0, 0))
    return pl.pallas_call(
        functools.partial(_lru_kernel, t_len=t_len),
        grid=(nb,),
        in_specs=[tok, tok, whole(cw), whole(cb), whole(wa), whole(ba), whole(wx), whole(bx),
                  whole(lam), st],
        out_specs=[pl.BlockSpec((t_len, W_C), lambda b: (b, 0)), st],
        out_shape=[SDS((nb * t_len, W_C), F32), SDS((nb, 2, W_C), F32)],
        compiler_params=_cparams("arbitrary"), name="lru",
    )(xc, gc, cw, cb, wa, ba, wx, bx, lam, s0)


def _mid_kernel(x_ref, mod_ref, of_ref, ob_ref, bonus_ref, g_ref, at_ref, y_ref, lng_ref, bd_ref,
                wout_ref, g2_ref, wqh_ref, wql_ref, x1_o, h2_o, q_o):
    bd = bd_ref[...]
    o = of_ref[...] + ob_ref[...]
    inv_n = 1.0 / HEAD_DIM
    mu = jnp.dot(o, bd, precision=HI, preferred_element_type=F32) * inv_n
    xc = o - mu
    var = jnp.dot(xc * xc, bd, precision=HI, preferred_element_type=F32) * inv_n
    out_a = (xc * lax.rsqrt(var + GN_EPS) * lng_ref[...] + bonus_ref[...]) * g_ref[...]
    cat = jnp.concatenate([out_a] + [at_ref[hh] for hh in range(H_B)] + [y_ref[...]], axis=-1)
    ymix = jnp.dot(cat.astype(BF16), wout_ref[...], preferred_element_type=F32)
    x1 = x_ref[...] + mod_ref[0, 2:3, :] * ymix
    x1_o[...] = x1
    xn = x1 * lax.rsqrt(jnp.mean(x1 * x1, axis=-1, keepdims=True) + RMS_EPS) * g2_ref[...]
    h2 = xn * (1.0 + mod_ref[0, 4:5, :]) + mod_ref[0, 3:4, :]
    h2_hi = h2.astype(BF16)
    h2_lo = (h2 - h2_hi.astype(F32)).astype(BF16)
    h2_o[...] = h2_hi
    wq_hi = wqh_ref[...]
    q_o[...] = (jnp.dot(h2_hi, wq_hi, preferred_element_type=F32)
                + (jnp.dot(h2_hi, wql_ref[...], preferred_element_type=F32)
                   + jnp.dot(h2_lo, wq_hi, preferred_element_type=F32)))


def _mid(x, mod, o_f, o_b, bonus, g, att, y, lng, bd, wout, g2, wq_hi, wq_lo):
    nb = N_TOK // TM

    def whole(a):
        return pl.BlockSpec(a.shape, lambda i: (0,) * a.ndim)

    tok = lambda w: pl.BlockSpec((TM, w), lambda i: (i, 0))
    return pl.pallas_call(
        _mid_kernel, grid=(nb,),
        in_specs=[tok(D_MODEL), pl.BlockSpec((1, 6, D_MODEL), _mod_index(TM)), tok(W_A), tok(W_A),
                  tok(W_A), tok(W_A), pl.BlockSpec((H_B, TM, HEAD_DIM), lambda i: (0, i, 0)),
                  tok(W_C), whole(lng), whole(bd), whole(wout), whole(g2), whole(wq_hi), whole(wq_lo)],
        out_specs=[tok(D_MODEL), tok(D_MODEL), tok(PEER_HEADS * PEER_DQ)],
        out_shape=[SDS((N_TOK, D_MODEL), F32), SDS((N_TOK, D_MODEL), BF16),
                   SDS((N_TOK, PEER_HEADS * PEER_DQ), F32)],
        compiler_params=_cparams("arbitrary"), name="mid",
    )(x, mod, o_f, o_b, bonus, g, att, y, lng, bd, wout, g2, wq_hi, wq_lo)


def _topk_kernel(q_ref, keys_ref, r2_o, k1_o, e1_o, e2_o, t_scr, i_scr):
    tn = q_ref.shape[0]
    kio = lax.broadcasted_iota(jnp.int32, (PEER_N_KEYS, tn), 0).astype(F32)
    aio = lax.broadcasted_iota(jnp.int32, (PEER_TOPK, tn), 0).astype(F32)
    neg_inf = -jnp.inf
    scores = []
    rank2 = jnp.full((PEER_N_KEYS, tn), float(PEER_TOPK), F32)
    for c in range(2):
        half = PEER_DQ // 2
        st = _dot_nt(keys_ref[0, c], q_ref[:, c * half:(c + 1) * half], precision=HI)
        x = st
        for kth in range(PEER_TOPK):
            m = jnp.max(x, axis=0, keepdims=True)
            first = jnp.min(jnp.where(x == m, kio, float(PEER_N_KEYS)), axis=0, keepdims=True)
            hit = kio == first
            x = jnp.where(hit, neg_inf, x)
            t_scr[c, kth:kth + 1, :] = m
            if c == 0:
                i_scr[kth:kth + 1, :] = first
            else:
                rank2 = jnp.where(hit, float(kth), rank2)
        scores.append(st)
    t1 = t_scr[0]
    t2 = t_scr[1]
    m0 = t1[0:1, :] + t2[0:1, :]
    f = t1 + t2[0:1, :]
    cnt = jnp.zeros((PEER_TOPK, tn), F32)
    z = jnp.zeros((1, tn), F32)
    for _ in range(PEER_TOPK):
        m = jnp.max(f, axis=0, keepdims=True)
        first = jnp.min(jnp.where(f == m, aio, float(PEER_TOPK)), axis=0, keepdims=True)
        hit = aio == first
        z = z + jnp.exp(m - m0)
        cnt = jnp.where(hit, cnt + 1.0, cnt)
        ch = jnp.sum(jnp.where(hit, cnt, 0.0), axis=0, keepdims=True)
        nt2 = jnp.sum(jnp.where(aio == ch, t2, 0.0), axis=0, keepdims=True)
        newf = jnp.where(ch >= float(PEER_TOPK), neg_inf, t1 + nt2)
        f = jnp.where(hit, newf, f)
    quota = jnp.zeros((PEER_N_KEYS, tn), F32)
    for a in range(PEER_TOPK):
        quota = jnp.where(kio == i_scr[a:a + 1, :], cnt[a:a + 1, :], quota)
    r2_o[0] = rank2.astype(BF16)
    k1_o[0] = quota
    e1_o[0] = jnp.exp(scores[0] - t1[0:1, :]) * (1.0 / z)
    e2_o[0] = jnp.exp(scores[1] - t2[0:1, :]).astype(BF16)


def _peer_topk(q, keys):
    tn = TN_PEER
    shape = (PEER_HEADS, PEER_N_KEYS, N_TOK)
    ospec = pl.BlockSpec((1, PEER_N_KEYS, tn), lambda i, h: (h, 0, i))
    return pl.pallas_call(
        _topk_kernel, grid=(N_TOK // tn, PEER_HEADS),
        in_specs=[pl.BlockSpec((tn, PEER_DQ), lambda i, h: (i, h)),
                  pl.BlockSpec((1, 2, PEER_N_KEYS, PEER_DQ // 2), lambda i, h: (h, 0, 0, 0))],
        out_specs=[ospec] * 4,
        out_shape=[SDS(shape, BF16), SDS(shape, F32), SDS(shape, F32), SDS(shape, BF16)],
        scratch_shapes=[pltpu.VMEM((2, PEER_TOPK, tn), F32), pltpu.VMEM((PEER_TOPK, tn), F32)],
        compiler_params=_cparams("arbitrary", "arbitrary"), name="peer_topk",
    )(q, keys)


def _dense_kernel(h2_ref, u_ref, vt_ref, r2_ref, e2_ref, k1_ref, e1_ref, x1_ref, mod_ref, fg_ref,
                  o_ref, acc, act_scr, p_scr, *, final):
    j = pl.program_id(1)

    @pl.when(j == 0)
    def _():
        acc[...] = jnp.zeros_like(acc)

    act_scr[...] = _gelu(_dot_nt(u_ref[...], h2_ref[...])).astype(BF16)
    tile = (BF16_SUBLANES, 128)
    for lt in range(h2_ref.shape[0] // 128):
        ls = slice(lt * 128, (lt + 1) * 128)
        for r in range(ROWS_PEER):
            quota = [jnp.broadcast_to(k1_ref[hh, r:r + 1, ls], tile).astype(BF16)
                     for hh in range(PEER_HEADS)]
            e1 = [jnp.broadcast_to(e1_ref[hh, r:r + 1, ls], tile).astype(BF16)
                  for hh in range(PEER_HEADS)]
            for blk in range(PEER_N_KEYS // BF16_SUBLANES):
                rs = slice(blk * BF16_SUBLANES, (blk + 1) * BF16_SUBLANES)
                gate = None
                for hh in range(PEER_HEADS):
                    term = jnp.where(r2_ref[hh, rs, ls] < quota[hh], e2_ref[hh, rs, ls] * e1[hh], 0.0)
                    gate = term if gate is None else gate + term
                es = slice(r * PEER_N_KEYS + blk * BF16_SUBLANES,
                           r * PEER_N_KEYS + (blk + 1) * BF16_SUBLANES)
                p_scr[es, ls] = gate * act_scr[es, ls]
    acc[...] += jnp.dot(vt_ref[...], p_scr[...], preferred_element_type=F32)

    @pl.when(j == pl.num_programs(1) - 1)
    def _():
        x2 = x1_ref[...] + mod_ref[0, 5:6, :] * acc[...].T
        if final:
            x2 = x2 * lax.rsqrt(jnp.mean(x2 * x2, axis=-1, keepdims=True) + RMS_EPS) * fg_ref[...]
        o_ref[...] = x2


def _peer_dense(h2, u, vt, r2, e2, k1, e1, x1, mod, fg, *, final):
    tn = TN_PEER
    ne = ROWS_PEER * PEER_N_KEYS
    per_tok = pl.BlockSpec((PEER_HEADS, PEER_N_KEYS, tn), lambda i, j: (0, 0, i))
    per_row = pl.BlockSpec((PEER_HEADS, ROWS_PEER, tn), lambda i, j: (0, j, i))
    mod_idx = _mod_index(tn)
    return pl.pallas_call(
        functools.partial(_dense_kernel, final=final),
        grid=(N_TOK // tn, N_EXPERTS // ne),
        in_specs=[pl.BlockSpec((tn, D_MODEL), lambda i, j: (i, 0)),
                  pl.BlockSpec((ne, D_MODEL), lambda i, j: (j, 0)),
                  pl.BlockSpec((D_MODEL, ne), lambda i, j: (0, j)),
                  per_tok, per_tok, per_row, per_row,
                  pl.BlockSpec((tn, D_MODEL), lambda i, j: (i, 0)),
                  pl.BlockSpec((1, 6, D_MODEL), lambda i, j: mod_idx(i)),
                  pl.BlockSpec((1, D_MODEL), lambda i, j: (0, 0))],
        out_specs=pl.BlockSpec((tn, D_MODEL), lambda i, j: (i, 0)),
        out_shape=SDS((N_TOK, D_MODEL), F32),
        scratch_shapes=[pltpu.VMEM((D_MODEL, tn), F32), pltpu.VMEM((ne, tn), BF16),
                        pltpu.VMEM((ne, tn), BF16)],
        compiler_params=_cparams("arbitrary", "arbitrary"), name="peer_dense",
    )(h2, u, vt, r2, e2, k1, e1, x1, mod, fg)


def _block_diag(blocks):
    g, n, m = blocks.shape
    eye = jnp.eye(g, dtype=blocks.dtype)
    return (eye[:, None, :, None] * blocks[:, :, None, :]).reshape(g * n, g * m)


def _rope_tables():
    rows = DEC_SEQ // GRID_W
    row = jnp.repeat(jnp.arange(rows), GRID_W).astype(F32)
    col = jnp.tile(jnp.arange(GRID_W), rows).astype(F32)
    inv = ROPE_BASE ** (-jnp.arange(ROPE_PAIRS, dtype=F32) / ROPE_PAIRS)
    ang = jnp.concatenate([row[:, None] * inv, col[:, None] * inv], axis=-1)
    lane = jnp.arange(W_B)
    e = (lane % HEAD_DIM) % DK
    pair, parity = e // 2, e % 2
    cos = jnp.cos(ang)[:, pair]
    sin = jnp.sin(ang)[:, pair]
    sa = jnp.where(parity == 0, -sin, 0.0)
    sb = jnp.where(parity == 1, sin, 0.0)
    ident = jnp.ones((TM, W_B), F32)
    zero = jnp.zeros((TM, W_B), F32)
    return (jnp.concatenate([cos, ident]), jnp.concatenate([sa, zero]), jnp.concatenate([sb, zero]))


def kernel(x_prompt, x_sample, c, cache_k, cache_v, state_rwkv, state_lru, c_ctx, mod_w, mod_b, norm1_g, norm2_g, w_in, w_out, rwkv_w0, rwkv_wA, rwkv_wB, rwkv_a0, rwkv_aA, rwkv_aB, rwkv_gA, rwkv_gB, rwkv_kk, rwkv_ka, rwkv_rk, rwkv_ln_g, diff_lq1, diff_lk1, diff_lq2, diff_lk2, diff_subln_g, lru_conv_w, lru_conv_b, lru_wa, lru_ba, lru_wx, lru_bx, lru_lambda, peer_wq, peer_keys, peer_u, peer_v, final_norm_g):
    x = jnp.concatenate([x_prompt.reshape(N_CTX, D_MODEL), x_sample.reshape(N_LAT, D_MODEL)])
    cond = jnp.concatenate([c_ctx[None], c, jnp.zeros((16 - 1 - DEC_BATCH, D_MODEL), F32)])
    cos_t, sa_t, sb_t = _rope_tables()
    head_id = jnp.arange(W_A) // HEAD_DIM
    bd = (head_id[:, None] == head_id[None, :]).astype(F32)
    zero_rwkv = jnp.zeros((2, HEAD_DIM, HEAD_DIM, BATCH * H_A), F32)
    zero_lru = jnp.zeros((BATCH, 2, W_C), F32)
    i_hi_lat = 128 // (DEC_BATCH * H_A)
    fg = final_norm_g[None]

    ks_list, vs_list, sr_list, sl_list = [], [], [], []
    for l in range(DEPTH):
        lam_init = 0.8 - 0.6 * math.exp(-0.3 * l)
        lam = (jnp.exp(jnp.sum(diff_lq1[l] * diff_lk1[l])) - jnp.exp(jnp.sum(diff_lq2[l] * diff_lk2[l]))
               + lam_init).reshape(1)
        mod = _adaln(cond, mod_w[l], mod_b[l])[:1 + DEC_BATCH].reshape(1 + DEC_BATCH, 6, D_MODEL)

        wcat = jnp.concatenate([w_in[l], rwkv_wA[l, 0], rwkv_wA[l, 1], rwkv_aA[l, 0], rwkv_aA[l, 1],
                                rwkv_gA[l]], axis=1).astype(BF16)
        (skeys, v, bonus, g, q_h, k_h, v_h, xc, gc) = _proj(
            x, mod, norm1_g[l][None], wcat,
            _block_diag(rwkv_wB[l]).astype(BF16), _block_diag(rwkv_aB[l]).astype(BF16),
            rwkv_gB[l].astype(BF16), rwkv_w0[l].reshape(1, 2 * W_A), rwkv_a0[l].reshape(1, 2 * W_A),
            rwkv_kk[l][None], rwkv_ka[l][None], rwkv_rk[l].reshape(1, W_A), bd, cos_t, sa_t, sb_t)

        o_f, o_b = [], []
        for (lo, nb, t_len, i_hi, s0) in (
                (0, BATCH, SEQ, 1, zero_rwkv),
                (N_CTX, DEC_BATCH, DEC_SEQ, i_hi_lat, _state_to_scan(state_rwkv[:, l], DEC_BATCH, i_hi_lat))):
            sl = slice(lo, lo + nb * t_len)
            of_s, ob_s, s_last = _rwkv_scan(_to_scan_keys(skeys[sl], nb, t_len, i_hi),
                                            _to_scan_vals(v[sl], nb, t_len, i_hi), s0)
            o_f.append(_from_scan_vals(of_s, nb, t_len, i_hi))
            o_b.append(_from_scan_vals(ob_s, nb, t_len, i_hi))
            if lo == 0:
                sr_list.append(_state_from_scan(s_last, nb, i_hi))
        o_f = jnp.concatenate(o_f)
        o_b = jnp.concatenate(o_b)

        sub_g = diff_subln_g[l][None]
        att_p = _attention(lam, q_h, k_h, v_h, sub_g, lam_init, nb=BATCH, t_len=SEQ, row_off=0)
        att_s = _attention(lam, q_h, k_h, v_h, sub_g, lam_init, nb=DEC_BATCH, t_len=DEC_SEQ,
                           row_off=N_CTX, layer=l, ck=cache_k, cv=cache_v)
        att = jnp.concatenate([att_p, att_s], axis=1)
        ks_list.append(k_h[:, :N_CTX].reshape(H_B, BATCH, SEQ, HEAD_DIM).transpose(1, 0, 2, 3))
        vs_list.append(v_h[:, :N_CTX].reshape(H_B, BATCH, SEQ, HEAD_DIM).transpose(1, 0, 2, 3))

        lru_args = (lru_conv_w[l], lru_conv_b[l][None], jax.vmap(_block_diag)(lru_wa[l]), lru_ba[l],
                    jax.vmap(_block_diag)(lru_wx[l]), lru_bx[l], lru_lambda[l])
        y_p, hl_p = _lru(xc, gc, *lru_args, zero_lru, nb=BATCH, t_len=SEQ, row_off=0)
        y_s, _ = _lru(xc, gc, *lru_args, state_lru[:, l], nb=DEC_BATCH, t_len=DEC_SEQ, row_off=N_CTX)
        sl_list.append(hl_p)
        y = jnp.concatenate([y_p, y_s])

        wq_hi = peer_wq[l].astype(BF16)
        wq_lo = (peer_wq[l] - wq_hi.astype(F32)).astype(BF16)
        x1, h2, q = _mid(x, mod, o_f, o_b, bonus, g, att, y, rwkv_ln_g[l][None], bd,
                         w_out[l].astype(BF16), norm2_g[l][None], wq_hi, wq_lo)
        r2, k1, e1, e2 = _peer_topk(q, peer_keys[l])
        x = _peer_dense(h2, peer_u[l].astype(BF16), peer_v[l].T.astype(BF16), r2, e2, k1, e1, x1, mod, fg,
                        final=(l == DEPTH - 1))

    y_prompt = x[:N_CTX].reshape(BATCH, SEQ, D_MODEL)
    y_sample = x[N_CTX:].reshape(DEC_BATCH, DEC_SEQ, D_MODEL)
    return (y_prompt, y_sample, jnp.stack(ks_list, axis=1), jnp.stack(vs_list, axis=1),
            jnp.stack(sr_list, axis=1), jnp.stack(sl_list, axis=1))
```

```python
import functools
import math

import jax
import jax.numpy as jnp
from jax import lax
from jax.experimental import pallas as pl
from jax.experimental.pallas import tpu as pltpu

F32 = jnp.float32
BF16 = jnp.bfloat16
HI = lax.Precision.HIGHEST
SDS = jax.ShapeDtypeStruct

D_MODEL = 1024
BATCH = 32
SEQ = 256
DEPTH = 2
DEC_BATCH = 8
DEC_SEQ = 1024
PAST_LEN = 512
GRID_W = 64
HEAD_DIM = 64
W_A = D_MODEL // 4
W_B = D_MODEL // 2
W_C = D_MODEL - W_A - W_B
H_A = W_A // HEAD_DIM
H_B = W_B // HEAD_DIM
DK = HEAD_DIM // 2
ROPE_PAIRS = DK // 4
ROPE_BASE = 10000.0
DECAY_SCALE = 0.606531
LORA_W = 64
LORA_G = 128
CONV_W = 4
CONV_LEFT = 2
LRU_C = 8.0
LRU_BLOCKS = W_C // HEAD_DIM
PEER_HEADS = 8
PEER_N_KEYS = 128
PEER_TOPK = 16
PEER_DQ = 256
N_EXPERTS = PEER_N_KEYS ** 2
D_IN = 3 * W_A + 3 * W_B + 2 * W_C
D_CAT = D_IN + 4 * LORA_W + LORA_G
RMS_EPS = 1e-6
GN_EPS = 64e-5
SUBLN_EPS = 1e-5
INV_SQRT2 = 0.7071067811865476
LOG2_E = 1.4426950408889634

N_CTX = BATCH * SEQ
N_LAT = DEC_BATCH * DEC_SEQ
N_TOK = N_CTX + N_LAT

VMEM_LIMIT_BYTES = 56 * 1024 * 1024
TM = 256
TQ = 256
TC_SCAN = 16
TN_PEER = 512
ROWS_PEER = 8
N_SCAN_KEYS = 8
BF16_SUBLANES = 16


def _cparams(*sem):
    return pltpu.CompilerParams(dimension_semantics=sem, vmem_limit_bytes=VMEM_LIMIT_BYTES)


def _mod_index(block_tokens):
    nbc = N_CTX // block_tokens
    per_seq = DEC_SEQ // block_tokens
    return lambda i: (jnp.where(i < nbc, 0, 1 + (i - nbc) // per_seq), 0, 0)


def _gelu(x):
    return 0.5 * x * (1.0 + lax.erf(x * INV_SQRT2))


def _dot_nt(a, b, precision=None):
    return lax.dot_general(a, b, (((1,), (1,)), ((), ())), precision=precision,
                           preferred_element_type=F32)


def _adaln_kernel(c_ref, w_ref, b_ref, o_ref):
    c = c_ref[...]
    s = c * jax.nn.sigmoid(c)
    o_ref[...] = jnp.dot(s, w_ref[...], precision=HI, preferred_element_type=F32) + b_ref[...]


def _adaln(cond, w, b):
    rows, n = cond.shape[0], w.shape[1]
    tn = 1536
    return pl.pallas_call(
        _adaln_kernel,
        grid=(n // tn,),
        in_specs=[pl.BlockSpec((rows, D_MODEL), lambda j: (0, 0)),
                  pl.BlockSpec((D_MODEL, tn), lambda j: (0, j)),
                  pl.BlockSpec((1, tn), lambda j: (0, j))],
        out_specs=pl.BlockSpec((rows, tn), lambda j: (0, j)),
        out_shape=SDS((rows, n), F32),
        compiler_params=_cparams("arbitrary"),
        name="adaln",
    )(cond, w, b[None])


def _proj_kernel(x_ref, mod_ref, g1_ref, wcat_ref, wb_ref, ab_ref, gb_ref, w0_ref, a0_ref,
                 kkw_ref, ka_ref, rk_ref, bd_ref, cos_ref, sa_ref, sb_ref,
                 keys_o, v_o, bonus_o, g_o, q_o, k_o, vv_o, xc_o, gc_o):
    x = x_ref[...]
    xn = x * lax.rsqrt(jnp.mean(x * x, axis=-1, keepdims=True) + RMS_EPS) * g1_ref[...]
    h = xn * (1.0 + mod_ref[0, 1:2, :]) + mod_ref[0, 0:1, :]
    p = jnp.dot(h.astype(BF16), wcat_ref[...], preferred_element_type=F32)

    r = p[:, 0:W_A]
    k = p[:, W_A:2 * W_A]
    v = p[:, 2 * W_A:3 * W_A]
    o = 3 * W_A
    q_b = p[:, o:o + W_B]
    k_b = p[:, o + W_B:o + 2 * W_B]
    v_b = p[:, o + 2 * W_B:o + 3 * W_B]
    o += 3 * W_B
    xc_o[...] = p[:, o:o + W_C]
    gc_o[...] = p[:, o + W_C:o + 2 * W_C]
    o = D_IN
    lw = jnp.tanh(p[:, o:o + 2 * LORA_W])
    la = p[:, o + 2 * LORA_W:o + 4 * LORA_W]
    lg = jax.nn.sigmoid(p[:, o + 4 * LORA_W:o + 4 * LORA_W + LORA_G])

    wpre = jnp.dot(lw.astype(BF16), wb_ref[...], preferred_element_type=F32) + w0_ref[...]
    apre = jnp.dot(la.astype(BF16), ab_ref[...], preferred_element_type=F32) + a0_ref[...]
    wdec = jnp.exp(-DECAY_SCALE * jax.nn.sigmoid(wpre))
    aicl = jax.nn.sigmoid(apre)
    g_o[...] = jnp.dot(lg.astype(BF16), gb_ref[...], preferred_element_type=F32)

    bd = bd_ref[...]
    kq = k * kkw_ref[...]
    ss = jnp.dot(kq * kq, bd, precision=HI, preferred_element_type=F32)
    kk = kq * lax.rsqrt(jnp.maximum(ss, 1e-12))
    keys_o[:, 0:W_A] = -kk
    keys_o[:, W_A:2 * W_A] = r
    v_o[...] = v
    ka = ka_ref[...]
    for d in range(2):
        a_d = aicl[:, d * W_A:(d + 1) * W_A]
        base = (2 + 3 * d) * W_A
        keys_o[:, base:base + W_A] = wdec[:, d * W_A:(d + 1) * W_A]
        keys_o[:, base + W_A:base + 2 * W_A] = kk * a_d
        keys_o[:, base + 2 * W_A:base + 3 * W_A] = k * (1.0 + (a_d - 1.0) * ka)
    bonus_o[...] = jnp.dot(r * k * rk_ref[...], bd, precision=HI, preferred_element_type=F32) * v

    cos, sa, sb = cos_ref[...], sa_ref[...], sb_ref[...]

    def rope(z):
        return z * cos + pltpu.roll(z, W_B - 1, 1) * sa + pltpu.roll(z, 1, 1) * sb

    q_r = rope(q_b)
    k_r = rope(k_b)
    for hh in range(H_B):
        sl = slice(hh * HEAD_DIM, (hh + 1) * HEAD_DIM)
        q_o[hh] = q_r[:, sl]
        k_o[hh] = k_r[:, sl]
        vv_o[hh] = v_b[:, sl]


def _proj(x, mod, g1, wcat, wb, ab, gb, w0, a0, kkw, ka, rk, bd, cos_t, sa_t, sb_t):
    nb = N_TOK // TM
    nbc = N_CTX // TM
    per_seq = DEC_SEQ // TM
    ident_blk = per_seq

    def rope_idx(i):
        return (jnp.where(i < nbc, ident_blk, (i - nbc) % per_seq), 0)

    def whole(a):
        return pl.BlockSpec(a.shape, lambda i: (0,) * a.ndim)

    tok = lambda w: pl.BlockSpec((TM, w), lambda i: (i, 0))
    heads = pl.BlockSpec((H_B, TM, HEAD_DIM), lambda i: (0, i, 0))
    in_specs = [tok(D_MODEL), pl.BlockSpec((1, 6, D_MODEL), _mod_index(TM)), whole(g1), whole(wcat),
                whole(wb), whole(ab), whole(gb), whole(w0), whole(a0), whole(kkw), whole(ka),
                whole(rk), whole(bd),
                pl.BlockSpec((TM, W_B), rope_idx), pl.BlockSpec((TM, W_B), rope_idx),
                pl.BlockSpec((TM, W_B), rope_idx)]
    out_specs = [tok(N_SCAN_KEYS * W_A)] + [tok(W_A)] * 3 + [heads] * 3 + [tok(W_C)] * 2
    out_shape = ([SDS((N_TOK, N_SCAN_KEYS * W_A), F32)] + [SDS((N_TOK, W_A), F32)] * 3
                 + [SDS((H_B, N_TOK, HEAD_DIM), F32)] * 3 + [SDS((N_TOK, W_C), F32)] * 2)
    return pl.pallas_call(
        _proj_kernel, grid=(nb,), in_specs=in_specs, out_specs=out_specs, out_shape=out_shape,
        compiler_params=_cparams("arbitrary"), name="proj",
    )(x, mod, g1, wcat, wb, ab, gb, w0, a0, kkw, ka, rk, bd, cos_t, sa_t, sb_t)


def _rwkv_kernel(keys_f, vf, keys_b, vb, s0_ref, of_ref, ob_ref, sl_ref, s_scr, *, tc, groups):
    i = pl.program_id(0)
    n_slabs = HEAD_DIM // groups

    @pl.when(i == 0)
    def _():
        s_scr[...] = s0_ref[...]

    nacc = 4

    def total(parts):
        x = (parts[0] + parts[1]) + (parts[2] + parts[3])
        width = 128
        while width > 128 // groups:
            width //= 2
            x = x + pltpu.roll(x, width, 1)
        return x

    def one_dir(d, t, keys, v, o):
        nk, r, w, b, k = 0, 1, 2 + 3 * d, 3 + 3 * d, 4 + 3 * d

        def row(a, j):
            return keys[t, a, pl.ds(j, 1), :]

        parts = [None] * nacc
        for j in range(n_slabs):
            term = s_scr[d, j] * row(nk, j)
            parts[j % nacc] = term if parts[j % nacc] is None else parts[j % nacc] + term
        sa = total(parts)
        vt = v[t]
        parts = [None] * nacc
        for j in range(n_slabs):
            sj = s_scr[d, j] * row(w, j) + sa * row(b, j) + vt * row(k, j)
            s_scr[d, j] = sj
            term = sj * row(r, j)
            parts[j % nacc] = term if parts[j % nacc] is None else parts[j % nacc] + term
        o[t] = total(parts)

    def step(tt, carry):
        one_dir(0, tt, keys_f, vf, of_ref)
        one_dir(1, tc - 1 - tt, keys_b, vb, ob_ref)
        return carry

    lax.fori_loop(0, tc, step, 0)

    @pl.when(i == pl.num_programs(0) - 1)
    def _():
        sl_ref[...] = s_scr[...]


def _rwkv_scan(keys, v, s0):
    t_len, _, n_slabs, _ = keys.shape
    tc = TC_SCAN
    nt = t_len // tc
    fwd = pl.BlockSpec((tc, HEAD_DIM, 128), lambda i: (i, 0, 0))
    bwd = pl.BlockSpec((tc, HEAD_DIM, 128), lambda i: (nt - 1 - i, 0, 0))
    kfwd = pl.BlockSpec((tc, N_SCAN_KEYS, n_slabs, 128), lambda i: (i, 0, 0, 0))
    kbwd = pl.BlockSpec((tc, N_SCAN_KEYS, n_slabs, 128), lambda i: (nt - 1 - i, 0, 0, 0))
    st = pl.BlockSpec((2, n_slabs, HEAD_DIM, 128), lambda i: (0, 0, 0, 0))
    return pl.pallas_call(
        functools.partial(_rwkv_kernel, tc=tc, groups=HEAD_DIM // n_slabs),
        grid=(nt,),
        in_specs=[kfwd, fwd, kbwd, bwd, st],
        out_specs=[fwd, bwd, st],
        out_shape=[SDS((t_len, HEAD_DIM, 128), F32), SDS((t_len, HEAD_DIM, 128), F32),
                   SDS((2, n_slabs, HEAD_DIM, 128), F32)],
        scratch_shapes=[pltpu.VMEM((2, n_slabs, HEAD_DIM, 128), F32)],
        compiler_params=_cparams("arbitrary"), name="rwkv_scan",
    )(keys, v, keys, v, s0)


def _scan_groups(nb):
    return 128 // (nb * H_A)


def _to_scan_keys(a, nb, t_len):
    a = a.reshape(nb, t_len, N_SCAN_KEYS, H_A, HEAD_DIM).transpose(1, 2, 4, 0, 3)
    return a.reshape(t_len, N_SCAN_KEYS, HEAD_DIM // _scan_groups(nb), 128)


def _to_scan_vals(a, nb, t_len):
    a = a.reshape(nb, t_len, H_A, HEAD_DIM).transpose(1, 3, 0, 2).reshape(t_len, HEAD_DIM, nb * H_A)
    return jnp.tile(a, (1, 1, _scan_groups(nb)))


def _from_scan_vals(o, nb, t_len):
    o = o[:, :, :nb * H_A].reshape(t_len, HEAD_DIM, nb, H_A).transpose(2, 0, 3, 1)
    return o.reshape(nb * t_len, W_A)


def _state_to_scan(s, nb):
    g = _scan_groups(nb)
    s = s.reshape(nb, 2, H_A, HEAD_DIM, HEAD_DIM // g, g).transpose(1, 4, 3, 5, 0, 2)
    return s.reshape(2, HEAD_DIM // g, HEAD_DIM, 128)


def _state_from_scan(s, nb):
    g = _scan_groups(nb)
    s = s.reshape(2, HEAD_DIM // g, HEAD_DIM, g, nb, H_A).transpose(4, 0, 5, 2, 1, 3)
    return s.reshape(nb, 2, H_A, HEAD_DIM, HEAD_DIM)


def _attn_kernel(*refs, has_ctx, out_scale):
    if has_ctx:
        lam_ref, q_ref, k_ref, v_ref, ck_ref, cv_ref, g_ref, o_ref = refs
    else:
        lam_ref, q_ref, k_ref, v_ref, g_ref, o_ref = refs
    lam = lam_ref[0]
    q = q_ref[0] * (DK ** -0.5 * LOG2_E)
    lane = lax.broadcasted_iota(jnp.int32, q.shape, 1)
    qs = [jnp.where(lane < DK, q, 0.0).astype(BF16), jnp.where(lane >= DK, q, 0.0).astype(BF16)]
    kn = k_ref[0].astype(BF16)
    vn = v_ref[0].astype(BF16)
    if has_ctx:
        kc = ck_ref[0, 0, 0].astype(BF16)
        vc = cv_ref[0, 0, 0].astype(BF16)
    o = None
    for c in range(2):
        s_n = _dot_nt(qs[c], kn)
        m = jnp.max(s_n, axis=-1, keepdims=True)
        if has_ctx:
            s_c = _dot_nt(qs[c], kc)
            m = jnp.maximum(m, jnp.max(s_c, axis=-1, keepdims=True))
        e_n = jnp.exp2(s_n - m)
        z = jnp.sum(e_n, axis=-1, keepdims=True)
        o_c = jnp.dot(e_n.astype(BF16), vn, preferred_element_type=F32)
        if has_ctx:
            e_c = jnp.exp2(s_c - m)
            z = z + jnp.sum(e_c, axis=-1, keepdims=True)
            o_c = o_c + jnp.dot(e_c.astype(BF16), vc, preferred_element_type=F32)
        coef = (1.0 / z) if c == 0 else (-lam / z)
        o = o_c * coef if o is None else o + o_c * coef
    o = o * lax.rsqrt(jnp.mean(o * o, axis=-1, keepdims=True) + SUBLN_EPS) * g_ref[...]
    o_ref[0] = o * out_scale


def _attention(lam, q, k, v, g, lam_init, *, nb, t_len, row_off, layer=None, ck=None, cv=None):
    has_ctx = ck is not None
    nq = t_len // TQ
    qoff = row_off // TQ
    koff = row_off // t_len
    in_specs = [pl.BlockSpec(memory_space=pltpu.SMEM),
                pl.BlockSpec((1, TQ, HEAD_DIM), lambda b, h, i: (h, qoff + b * nq + i, 0)),
                pl.BlockSpec((1, t_len, HEAD_DIM), lambda b, h, i: (h, koff + b, 0)),
                pl.BlockSpec((1, t_len, HEAD_DIM), lambda b, h, i: (h, koff + b, 0))]
    args = [lam, q, k, v]
    if has_ctx:
        cspec = pl.BlockSpec((1, 1, 1, PAST_LEN, HEAD_DIM), lambda b, h, i: (b, layer, h, 0, 0))
        in_specs += [cspec, cspec]
        args += [ck, cv]
    in_specs.append(pl.BlockSpec((1, HEAD_DIM), lambda b, h, i: (0, 0)))
    args.append(g)
    return pl.pallas_call(
        functools.partial(_attn_kernel, has_ctx=has_ctx, out_scale=1.0 - lam_init),
        grid=(nb, H_B, nq),
        in_specs=in_specs,
        out_specs=pl.BlockSpec((1, TQ, HEAD_DIM), lambda b, h, i: (h, b * nq + i, 0)),
        out_shape=SDS((H_B, nb * t_len, HEAD_DIM), F32),
        compiler_params=_cparams("arbitrary", "arbitrary", "arbitrary"),
        name="attn_ctx" if has_ctx else "attn",
    )(*args)


def _lru_kernel(xc_ref, gc_ref, cw_ref, cb_ref, wa_ref, ba_ref, wx_ref, bx_ref, lam_ref, s0_ref,
                y_ref, hl_ref, *, t_len):
    x = xc_ref[...]
    row = lax.broadcasted_iota(jnp.int32, x.shape, 0)

    def shifted(a, off, fill):
        if off == 0:
            return a
        if off < 0:
            return jnp.where(row >= -off, pltpu.roll(a, -off, 0), fill)
        return jnp.where(row < t_len - off, pltpu.roll(a, t_len - off, 0), fill)

    xconv = cb_ref[...]
    for j in range(CONV_W):
        xconv = xconv + shifted(x, j - CONV_LEFT, 0.0) * cw_ref[j:j + 1, :]
    hsum = None
    for d in range(2):
        gr = jax.nn.sigmoid(jnp.dot(xconv, wa_ref[d], precision=HI, preferred_element_type=F32)
                            + ba_ref[d:d + 1, :])
        gi = jax.nn.sigmoid(jnp.dot(xconv, wx_ref[d], precision=HI, preferred_element_type=F32)
                            + bx_ref[d:d + 1, :])
        nl = -lam_ref[d:d + 1, :]
        softplus = jnp.maximum(nl, 0.0) + jnp.log1p(jnp.exp(-jnp.abs(nl)))
        log_a = -LRU_C * gr * softplus
        a = jnp.exp(log_a)
        b = jnp.sqrt(1.0 - jnp.exp(2.0 * log_a)) * (gi * xconv)
        dist = 1
        while dist < t_len:
            off = -dist if d == 0 else dist
            a_sh = shifted(a, off, 1.0)
            b_sh = shifted(b, off, 0.0)
            b = a * b_sh + b
            a = a * a_sh
            dist *= 2
        hs = b + a * s0_ref[0, d:d + 1, :]
        last = t_len - 1 if d == 0 else 0
        hl_ref[0, d:d + 1, :] = hs[last:last + 1, :]
        hsum = hs if hsum is None else hsum + hs
    y_ref[...] = hsum * _gelu(gc_ref[...])


def _lru(xc, gc, cw, cb, wa, ba, wx, bx, lam, s0, *, nb, t_len, row_off):
    off = row_off // t_len

    def whole(a):
        return pl.BlockSpec(a.shape, lambda b: (0,) * a.ndim)

    tok = pl.BlockSpec((t_len, W_C), lambda b: (off + b, 0))
    st = pl.BlockSpec((1, 2, W_C), lambda b: (b, 0, 0))
    return pl.pallas_call(
        functools.partial(_lru_kernel, t_len=t_len),
        grid=(nb,),
        in_specs=[tok, tok, whole(cw), whole(cb), whole(wa), whole(ba), whole(wx), whole(bx),
                  whole(lam), st],
        out_specs=[pl.BlockSpec((t_len, W_C), lambda b: (b, 0)), st],
        out_shape=[SDS((nb * t_len, W_C), F32), SDS((nb, 2, W_C), F32)],
        compiler_params=_cparams("arbitrary"), name="lru",
    )(xc, gc, cw, cb, wa, ba, wx, bx, lam, s0)


def _mid_kernel(x_ref, mod_ref, of_ref, ob_ref, bonus_ref, g_ref, at_ref, y_ref, lng_ref, bd_ref,
                wout_ref, g2_ref, wqh_ref, wql_ref, x1_o, h2_o, q_o):
    bd = bd_ref[...]
    o = of_ref[...] + ob_ref[...]
    inv_n = 1.0 / HEAD_DIM
    mu = jnp.dot(o, bd, precision=HI, preferred_element_type=F32) * inv_n
    xc = o - mu
    var = jnp.dot(xc * xc, bd, precision=HI, preferred_element_type=F32) * inv_n
    out_a = (xc * lax.rsqrt(var + GN_EPS) * lng_ref[...] + bonus_ref[...]) * g_ref[...]
    cat = jnp.concatenate([out_a] + [at_ref[hh] for hh in range(H_B)] + [y_ref[...]], axis=-1)
    ymix = jnp.dot(cat.astype(BF16), wout_ref[...], preferred_element_type=F32)
    x1 = x_ref[...] + mod_ref[0, 2:3, :] * ymix
    x1_o[...] = x1
    xn = x1 * lax.rsqrt(jnp.mean(x1 * x1, axis=-1, keepdims=True) + RMS_EPS) * g2_ref[...]
    h2 = xn * (1.0 + mod_ref[0, 4:5, :]) + mod_ref[0, 3:4, :]
    h2_hi = h2.astype(BF16)
    h2_lo = (h2 - h2_hi.astype(F32)).astype(BF16)
    h2_o[...] = h2_hi
    wq_hi = wqh_ref[...]
    q_o[...] = (jnp.dot(h2_hi, wq_hi, preferred_element_type=F32)
                + (jnp.dot(h2_hi, wql_ref[...], preferred_element_type=F32)
                   + jnp.dot(h2_lo, wq_hi, preferred_element_type=F32)))


def _mid(x, mod, o_f, o_b, bonus, g, att, y, lng, bd, wout, g2, wq_hi, wq_lo):
    nb = N_TOK // TM

    def whole(a):
        return pl.BlockSpec(a.shape, lambda i: (0,) * a.ndim)

    tok = lambda w: pl.BlockSpec((TM, w), lambda i: (i, 0))
    return pl.pallas_call(
        _mid_kernel, grid=(nb,),
        in_specs=[tok(D_MODEL), pl.BlockSpec((1, 6, D_MODEL), _mod_index(TM)), tok(W_A), tok(W_A),
                  tok(W_A), tok(W_A), pl.BlockSpec((H_B, TM, HEAD_DIM), lambda i: (0, i, 0)),
                  tok(W_C), whole(lng), whole(bd), whole(wout), whole(g2), whole(wq_hi), whole(wq_lo)],
        out_specs=[tok(D_MODEL), tok(D_MODEL), tok(PEER_HEADS * PEER_DQ)],
        out_shape=[SDS((N_TOK, D_MODEL), F32), SDS((N_TOK, D_MODEL), BF16),
                   SDS((N_TOK, PEER_HEADS * PEER_DQ), F32)],
        compiler_params=_cparams("arbitrary"), name="mid",
    )(x, mod, o_f, o_b, bonus, g, att, y, lng, bd, wout, g2, wq_hi, wq_lo)


def _topk_kernel(q_ref, keys_ref, r2_o, k1_o, e1_o, e2_o, t_scr, i_scr):
    tn = q_ref.shape[0]
    kio = lax.broadcasted_iota(jnp.int32, (PEER_N_KEYS, tn), 0).astype(F32)
    aio = lax.broadcasted_iota(jnp.int32, (PEER_TOPK, tn), 0).astype(F32)
    neg_inf = -jnp.inf
    scores = []
    rank2 = jnp.full((PEER_N_KEYS, tn), float(PEER_TOPK), F32)
    for c in range(2):
        half = PEER_DQ // 2
        st = _dot_nt(keys_ref[0, c], q_ref[:, c * half:(c + 1) * half], precision=HI)
        x = st
        for kth in range(PEER_TOPK):
            m = jnp.max(x, axis=0, keepdims=True)
            first = jnp.min(jnp.where(x == m, kio, float(PEER_N_KEYS)), axis=0, keepdims=True)
            hit = kio == first
            x = jnp.where(hit, neg_inf, x)
            t_scr[c, kth:kth + 1, :] = m
            if c == 0:
                i_scr[kth:kth + 1, :] = first
            else:
                rank2 = jnp.where(hit, float(kth), rank2)
        scores.append(st)
    t1 = t_scr[0]
    t2 = t_scr[1]
    m0 = t1[0:1, :] + t2[0:1, :]
    f = t1 + t2[0:1, :]
    cnt = jnp.zeros((PEER_TOPK, tn), F32)
    z = jnp.zeros((1, tn), F32)
    for _ in range(PEER_TOPK):
        m = jnp.max(f, axis=0, keepdims=True)
        first = jnp.min(jnp.where(f == m, aio, float(PEER_TOPK)), axis=0, keepdims=True)
        hit = aio == first
        z = z + jnp.exp(m - m0)
        cnt = jnp.where(hit, cnt + 1.0, cnt)
        ch = jnp.sum(jnp.where(hit, cnt, 0.0), axis=0, keepdims=True)
        nt2 = jnp.sum(jnp.where(aio == ch, t2, 0.0), axis=0, keepdims=True)
        newf = jnp.where(ch >= float(PEER_TOPK), neg_inf, t1 + nt2)
        f = jnp.where(hit, newf, f)
    quota = jnp.zeros((PEER_N_KEYS, tn), F32)
    for a in range(PEER_TOPK):
        quota = jnp.where(kio == i_scr[a:a + 1, :], cnt[a:a + 1, :], quota)
    r2_o[0] = rank2.astype(BF16)
    k1_o[0] = quota
    e1_o[0] = jnp.exp(scores[0] - t1[0:1, :]) * (1.0 / z)
    e2_o[0] = jnp.exp(scores[1] - t2[0:1, :]).astype(BF16)


def _peer_topk(q, keys):
    tn = TN_PEER
    shape = (PEER_HEADS, PEER_N_KEYS, N_TOK)
    ospec = pl.BlockSpec((1, PEER_N_KEYS, tn), lambda i, h: (h, 0, i))
    return pl.pallas_call(
        _topk_kernel, grid=(N_TOK // tn, PEER_HEADS),
        in_specs=[pl.BlockSpec((tn, PEER_DQ), lambda i, h: (i, h)),
                  pl.BlockSpec((1, 2, PEER_N_KEYS, PEER_DQ // 2), lambda i, h: (h, 0, 0, 0))],
        out_specs=[ospec] * 4,
        out_shape=[SDS(shape, BF16), SDS(shape, F32), SDS(shape, F32), SDS(shape, BF16)],
        scratch_shapes=[pltpu.VMEM((2, PEER_TOPK, tn), F32), pltpu.VMEM((PEER_TOPK, tn), F32)],
        compiler_params=_cparams("arbitrary", "arbitrary"), name="peer_topk",
    )(q, keys)


def _dense_kernel(h2_ref, u_ref, vt_ref, r2_ref, e2_ref, k1_ref, e1_ref, x1_ref, mod_ref, fg_ref,
                  o_ref, acc, act_scr, p_scr, r2_scr, e2_scr, *, final):
    j = pl.program_id(1)

    @pl.when(j == 0)
    def _():
        acc[...] = jnp.zeros_like(acc)
        for hh in range(PEER_HEADS):
            r2_scr[hh * PEER_N_KEYS:(hh + 1) * PEER_N_KEYS, :] = r2_ref[hh]
            e2_scr[hh * PEER_N_KEYS:(hh + 1) * PEER_N_KEYS, :] = e2_ref[hh]

    act_scr[...] = _gelu(_dot_nt(u_ref[...], h2_ref[...])).astype(BF16)
    tile = (BF16_SUBLANES, 128)
    for lt in range(h2_ref.shape[0] // 128):
        ls = slice(lt * 128, (lt + 1) * 128)
        for r in range(ROWS_PEER):
            quota = [jnp.broadcast_to(k1_ref[hh, r:r + 1, ls], tile).astype(BF16)
                     for hh in range(PEER_HEADS)]
            e1 = [jnp.broadcast_to(e1_ref[hh, r:r + 1, ls], tile).astype(BF16)
                  for hh in range(PEER_HEADS)]
            for blk in range(PEER_N_KEYS // BF16_SUBLANES):
                gate = None
                for hh in range(PEER_HEADS):
                    rs = slice(hh * PEER_N_KEYS + blk * BF16_SUBLANES,
                               hh * PEER_N_KEYS + (blk + 1) * BF16_SUBLANES)
                    term = jnp.where(r2_scr[rs, ls] < quota[hh], e2_scr[rs, ls] * e1[hh], 0.0)
                    gate = term if gate is None else gate + term
                es = slice(r * PEER_N_KEYS + blk * BF16_SUBLANES,
                           r * PEER_N_KEYS + (blk + 1) * BF16_SUBLANES)
                p_scr[es, ls] = gate * act_scr[es, ls]
    acc[...] += jnp.dot(vt_ref[...], p_scr[...], preferred_element_type=F32)

    @pl.when(j == pl.num_programs(1) - 1)
    def _():
        x2 = x1_ref[...] + mod_ref[0, 5:6, :] * acc[...].T
        if final:
            x2 = x2 * lax.rsqrt(jnp.mean(x2 * x2, axis=-1, keepdims=True) + RMS_EPS) * fg_ref[...]
        o_ref[...] = x2


def _peer_dense(h2, u, vt, r2, e2, k1, e1, x1, mod, fg, *, final):
    tn = TN_PEER
    ne = ROWS_PEER * PEER_N_KEYS
    per_tok = pl.BlockSpec((PEER_HEADS, PEER_N_KEYS, tn), lambda i, j: (0, 0, i))
    per_row = pl.BlockSpec((PEER_HEADS, ROWS_PEER, tn), lambda i, j: (0, j, i))
    mod_idx = _mod_index(tn)
    return pl.pallas_call(
        functools.partial(_dense_kernel, final=final),
        grid=(N_TOK // tn, N_EXPERTS // ne),
        in_specs=[pl.BlockSpec((tn, D_MODEL), lambda i, j: (i, 0)),
                  pl.BlockSpec((ne, D_MODEL), lambda i, j: (j, 0)),
                  pl.BlockSpec((D_MODEL, ne), lambda i, j: (0, j)),
                  per_tok, per_tok, per_row, per_row,
                  pl.BlockSpec((tn, D_MODEL), lambda i, j: (i, 0)),
                  pl.BlockSpec((1, 6, D_MODEL), lambda i, j: mod_idx(i)),
                  pl.BlockSpec((1, D_MODEL), lambda i, j: (0, 0))],
        out_specs=pl.BlockSpec((tn, D_MODEL), lambda i, j: (i, 0)),
        out_shape=SDS((N_TOK, D_MODEL), F32),
        scratch_shapes=[pltpu.VMEM((D_MODEL, tn), F32), pltpu.VMEM((ne, tn), BF16),
                        pltpu.VMEM((ne, tn), BF16),
                        pltpu.VMEM((PEER_HEADS * PEER_N_KEYS, tn), BF16),
                        pltpu.VMEM((PEER_HEADS * PEER_N_KEYS, tn), BF16)],
        compiler_params=_cparams("arbitrary", "arbitrary"), name="peer_dense",
    )(h2, u, vt, r2, e2, k1, e1, x1, mod, fg)


def _block_diag(blocks):
    g, n, m = blocks.shape
    eye = jnp.eye(g, dtype=blocks.dtype)
    return (eye[:, None, :, None] * blocks[:, :, None, :]).reshape(g * n, g * m)


def _rope_tables():
    rows = DEC_SEQ // GRID_W
    row = jnp.repeat(jnp.arange(rows), GRID_W).astype(F32)
    col = jnp.tile(jnp.arange(GRID_W), rows).astype(F32)
    inv = ROPE_BASE ** (-jnp.arange(ROPE_PAIRS, dtype=F32) / ROPE_PAIRS)
    ang = jnp.concatenate([row[:, None] * inv, col[:, None] * inv], axis=-1)
    lane = jnp.arange(W_B)
    e = (lane % HEAD_DIM) % DK
    pair, parity = e // 2, e % 2
    cos = jnp.cos(ang)[:, pair]
    sin = jnp.sin(ang)[:, pair]
    sa = jnp.where(parity == 0, -sin, 0.0)
    sb = jnp.where(parity == 1, sin, 0.0)
    ident = jnp.ones((TM, W_B), F32)
    zero = jnp.zeros((TM, W_B), F32)
    return (jnp.concatenate([cos, ident]), jnp.concatenate([sa, zero]), jnp.concatenate([sb, zero]))


def kernel(x_prompt, x_sample, c, cache_k, cache_v, state_rwkv, state_lru, c_ctx, mod_w, mod_b, norm1_g, norm2_g, w_in, w_out, rwkv_w0, rwkv_wA, rwkv_wB, rwkv_a0, rwkv_aA, rwkv_aB, rwkv_gA, rwkv_gB, rwkv_kk, rwkv_ka, rwkv_rk, rwkv_ln_g, diff_lq1, diff_lk1, diff_lq2, diff_lk2, diff_subln_g, lru_conv_w, lru_conv_b, lru_wa, lru_ba, lru_wx, lru_bx, lru_lambda, peer_wq, peer_keys, peer_u, peer_v, final_norm_g):
    x = jnp.concatenate([x_prompt.reshape(N_CTX, D_MODEL), x_sample.reshape(N_LAT, D_MODEL)])
    cond = jnp.concatenate([c_ctx[None], c, jnp.zeros((16 - 1 - DEC_BATCH, D_MODEL), F32)])
    cos_t, sa_t, sb_t = _rope_tables()
    head_id = jnp.arange(W_A) // HEAD_DIM
    bd = (head_id[:, None] == head_id[None, :]).astype(F32)
    zero_rwkv = jnp.zeros((2, HEAD_DIM, HEAD_DIM, BATCH * H_A), F32)
    zero_lru = jnp.zeros((BATCH, 2, W_C), F32)
    fg = final_norm_g[None]

    ks_list, vs_list, sr_list, sl_list = [], [], [], []
    for l in range(DEPTH):
        lam_init = 0.8 - 0.6 * math.exp(-0.3 * l)
        lam = (jnp.exp(jnp.sum(diff_lq1[l] * diff_lk1[l])) - jnp.exp(jnp.sum(diff_lq2[l] * diff_lk2[l]))
               + lam_init).reshape(1)
        mod = _adaln(cond, mod_w[l], mod_b[l])[:1 + DEC_BATCH].reshape(1 + DEC_BATCH, 6, D_MODEL)

        wcat = jnp.concatenate([w_in[l], rwkv_wA[l, 0], rwkv_wA[l, 1], rwkv_aA[l, 0], rwkv_aA[l, 1],
                                rwkv_gA[l]], axis=1).astype(BF16)
        (skeys, v, bonus, g, q_h, k_h, v_h, xc, gc) = _proj(
            x, mod, norm1_g[l][None], wcat,
            _block_diag(rwkv_wB[l]).astype(BF16), _block_diag(rwkv_aB[l]).astype(BF16),
            rwkv_gB[l].astype(BF16), rwkv_w0[l].reshape(1, 2 * W_A), rwkv_a0[l].reshape(1, 2 * W_A),
            rwkv_kk[l][None], rwkv_ka[l][None], rwkv_rk[l].reshape(1, W_A), bd, cos_t, sa_t, sb_t)

        o_f, o_b = [], []
        for (lo, nb, t_len, s0) in (
                (0, BATCH, SEQ, zero_rwkv),
                (N_CTX, DEC_BATCH, DEC_SEQ, _state_to_scan(state_rwkv[:, l], DEC_BATCH))):
            sl = slice(lo, lo + nb * t_len)
            of_s, ob_s, s_last = _rwkv_scan(_to_scan_keys(skeys[sl], nb, t_len),
                                            _to_scan_vals(v[sl], nb, t_len), s0)
            o_f.append(_from_scan_vals(of_s, nb, t_len))
            o_b.append(_from_scan_vals(ob_s, nb, t_len))
            if lo == 0:
                sr_list.append(_state_from_scan(s_last, nb))
        o_f = jnp.concatenate(o_f)
        o_b = jnp.concatenate(o_b)

        sub_g = diff_subln_g[l][None]
        att_p = _attention(lam, q_h, k_h, v_h, sub_g, lam_init, nb=BATCH, t_len=SEQ, row_off=0)
        att_s = _attention(lam, q_h, k_h, v_h, sub_g, lam_init, nb=DEC_BATCH, t_len=DEC_SEQ,
                           row_off=N_CTX, layer=l, ck=cache_k, cv=cache_v)
        att = jnp.concatenate([att_p, att_s], axis=1)
        ks_list.append(k_h[:, :N_CTX].reshape(H_B, BATCH, SEQ, HEAD_DIM).transpose(1, 0, 2, 3))
        vs_list.append(v_h[:, :N_CTX].reshape(H_B, BATCH, SEQ, HEAD_DIM).transpose(1, 0, 2, 3))

        lru_args = (lru_conv_w[l], lru_conv_b[l][None], jax.vmap(_block_diag)(lru_wa[l]), lru_ba[l],
                    jax.vmap(_block_diag)(lru_wx[l]), lru_bx[l], lru_lambda[l])
        y_p, hl_p = _lru(xc, gc, *lru_args, zero_lru, nb=BATCH, t_len=SEQ, row_off=0)
        y_s, _ = _lru(xc, gc, *lru_args, state_lru[:, l], nb=DEC_BATCH, t_len=DEC_SEQ, row_off=N_CTX)
        sl_list.append(hl_p)
        y = jnp.concatenate([y_p, y_s])

        wq_hi = peer_wq[l].astype(BF16)
        wq_lo = (peer_wq[l] - wq_hi.astype(F32)).astype(BF16)
        x1, h2, q = _mid(x, mod, o_f, o_b, bonus, g, att, y, rwkv_ln_g[l][None], bd,
                         w_out[l].astype(BF16), norm2_g[l][None], wq_hi, wq_lo)
        r2, k1, e1, e2 = _peer_topk(q, peer_keys[l])
        x = _peer_dense(h2, peer_u[l].astype(BF16), peer_v[l].T.astype(BF16), r2, e2, k1, e1, x1, mod, fg,
                        final=(l == DEPTH - 1))

    y_prompt = x[:N_CTX].reshape(BATCH, SEQ, D_MODEL)
    y_sample = x[N_CTX:].reshape(DEC_BATCH, DEC_SEQ, D_MODEL)
    return (y_prompt, y_sample, jnp.stack(ks_list, axis=1), jnp.stack(vs_list, axis=1),
            jnp.stack(sr_list, axis=1), jnp.stack(sl_list, axis=1))
```

```python
import functools
import math

import jax
import jax.numpy as jnp
from jax import lax
from jax.experimental import pallas as pl
from jax.experimental.pallas import tpu as pltpu

F32 = jnp.float32
BF16 = jnp.bfloat16
HI = lax.Precision.HIGHEST
SDS = jax.ShapeDtypeStruct

D_MODEL = 1024
BATCH = 32
SEQ = 256
DEPTH = 2
DEC_BATCH = 8
DEC_SEQ = 1024
PAST_LEN = 512
GRID_W = 64
HEAD_DIM = 64
W_A = D_MODEL // 4
W_B = D_MODEL // 2
W_C = D_MODEL - W_A - W_B
H_A = W_A // HEAD_DIM
H_B = W_B // HEAD_DIM
DK = HEAD_DIM // 2
ROPE_PAIRS = DK // 4
ROPE_BASE = 10000.0
DECAY_SCALE = 0.606531
LORA_W = 64
LORA_G = 128
CONV_W = 4
CONV_LEFT = 2
LRU_C = 8.0
LRU_BLOCKS = W_C // HEAD_DIM
PEER_HEADS = 8
PEER_N_KEYS = 128
PEER_TOPK = 16
PEER_DQ = 256
N_EXPERTS = PEER_N_KEYS ** 2
D_IN = 3 * W_A + 3 * W_B + 2 * W_C
D_CAT = D_IN + 4 * LORA_W + LORA_G
RMS_EPS = 1e-6
GN_EPS = 64e-5
SUBLN_EPS = 1e-5
INV_SQRT2 = 0.7071067811865476
LOG2_E = 1.4426950408889634

N_CTX = BATCH * SEQ
N_LAT = DEC_BATCH * DEC_SEQ
N_TOK = N_CTX + N_LAT

VMEM_LIMIT_BYTES = 56 * 1024 * 1024
TM = 256
TQ = 256
TC_SCAN = 16
TN_PEER = 512
ROWS_PEER = 8
N_SCAN_KEYS = 8
N_FOLD_KEYS = 5
BF16_SUBLANES = 16


def _cparams(*sem):
    return pltpu.CompilerParams(dimension_semantics=sem, vmem_limit_bytes=VMEM_LIMIT_BYTES)


def _mod_index(block_tokens):
    nbc = N_CTX // block_tokens
    per_seq = DEC_SEQ // block_tokens
    return lambda i: (jnp.where(i < nbc, 0, 1 + (i - nbc) // per_seq), 0, 0)


def _gelu(x):
    return 0.5 * x * (1.0 + lax.erf(x * INV_SQRT2))


def _dot_nt(a, b, precision=None):
    return lax.dot_general(a, b, (((1,), (1,)), ((), ())), precision=precision,
                           preferred_element_type=F32)


def _adaln_kernel(c_ref, w_ref, b_ref, o_ref):
    c = c_ref[...]
    s = c * jax.nn.sigmoid(c)
    o_ref[...] = jnp.dot(s, w_ref[...], precision=HI, preferred_element_type=F32) + b_ref[...]


def _adaln(cond, w, b):
    rows, n = cond.shape[0], w.shape[1]
    tn = 1536
    return pl.pallas_call(
        _adaln_kernel,
        grid=(n // tn,),
        in_specs=[pl.BlockSpec((rows, D_MODEL), lambda j: (0, 0)),
                  pl.BlockSpec((D_MODEL, tn), lambda j: (0, j)),
                  pl.BlockSpec((1, tn), lambda j: (0, j))],
        out_specs=pl.BlockSpec((rows, tn), lambda j: (0, j)),
        out_shape=SDS((rows, n), F32),
        compiler_params=_cparams("arbitrary"),
        name="adaln",
    )(cond, w, b[None])


def _proj_kernel(x_ref, mod_ref, g1_ref, wcat_ref, wb_ref, ab_ref, gb_ref, w0_ref, a0_ref,
                 kkw_ref, ka_ref, rk_ref, bd_ref, cos_ref, sa_ref, sb_ref,
                 keys_o, v_o, bonus_o, g_o, q_o, k_o, vv_o, xc_o, gc_o):
    x = x_ref[...]
    xn = x * lax.rsqrt(jnp.mean(x * x, axis=-1, keepdims=True) + RMS_EPS) * g1_ref[...]
    h = xn * (1.0 + mod_ref[0, 1:2, :]) + mod_ref[0, 0:1, :]
    p = jnp.dot(h.astype(BF16), wcat_ref[...], preferred_element_type=F32)

    r = p[:, 0:W_A]
    k = p[:, W_A:2 * W_A]
    v = p[:, 2 * W_A:3 * W_A]
    o = 3 * W_A
    q_b = p[:, o:o + W_B]
    k_b = p[:, o + W_B:o + 2 * W_B]
    v_b = p[:, o + 2 * W_B:o + 3 * W_B]
    o += 3 * W_B
    xc_o[...] = p[:, o:o + W_C]
    gc_o[...] = p[:, o + W_C:o + 2 * W_C]
    o = D_IN
    lw = jnp.tanh(p[:, o:o + 2 * LORA_W])
    la = p[:, o + 2 * LORA_W:o + 4 * LORA_W]
    lg = jax.nn.sigmoid(p[:, o + 4 * LORA_W:o + 4 * LORA_W + LORA_G])

    wpre = jnp.dot(lw.astype(BF16), wb_ref[...], preferred_element_type=F32) + w0_ref[...]
    apre = jnp.dot(la.astype(BF16), ab_ref[...], preferred_element_type=F32) + a0_ref[...]
    wdec = jnp.exp(-DECAY_SCALE * jax.nn.sigmoid(wpre))
    aicl = jax.nn.sigmoid(apre)
    g_o[...] = jnp.dot(lg.astype(BF16), gb_ref[...], preferred_element_type=F32)

    bd = bd_ref[...]
    kq = k * kkw_ref[...]
    ss = jnp.dot(kq * kq, bd, precision=HI, preferred_element_type=F32)
    kk = kq * lax.rsqrt(jnp.maximum(ss, 1e-12))
    keys_o[:, 0:W_A] = -kk
    keys_o[:, W_A:2 * W_A] = r
    v_o[...] = v
    ka = ka_ref[...]
    for d in range(2):
        a_d = aicl[:, d * W_A:(d + 1) * W_A]
        base = (2 + 3 * d) * W_A
        keys_o[:, base:base + W_A] = wdec[:, d * W_A:(d + 1) * W_A]
        keys_o[:, base + W_A:base + 2 * W_A] = kk * a_d
        keys_o[:, base + 2 * W_A:base + 3 * W_A] = k * (1.0 + (a_d - 1.0) * ka)
    bonus_o[...] = jnp.dot(r * k * rk_ref[...], bd, precision=HI, preferred_element_type=F32) * v

    cos, sa, sb = cos_ref[...], sa_ref[...], sb_ref[...]

    def rope(z):
        return z * cos + pltpu.roll(z, W_B - 1, 1) * sa + pltpu.roll(z, 1, 1) * sb

    q_r = rope(q_b)
    k_r = rope(k_b)
    for hh in range(H_B):
        sl = slice(hh * HEAD_DIM, (hh + 1) * HEAD_DIM)
        q_o[hh] = q_r[:, sl]
        k_o[hh] = k_r[:, sl]
        vv_o[hh] = v_b[:, sl]


def _proj(x, mod, g1, wcat, wb, ab, gb, w0, a0, kkw, ka, rk, bd, cos_t, sa_t, sb_t):
    nb = N_TOK // TM
    nbc = N_CTX // TM
    per_seq = DEC_SEQ // TM
    ident_blk = per_seq

    def rope_idx(i):
        return (jnp.where(i < nbc, ident_blk, (i - nbc) % per_seq), 0)

    def whole(a):
        return pl.BlockSpec(a.shape, lambda i: (0,) * a.ndim)

    tok = lambda w: pl.BlockSpec((TM, w), lambda i: (i, 0))
    heads = pl.BlockSpec((H_B, TM, HEAD_DIM), lambda i: (0, i, 0))
    in_specs = [tok(D_MODEL), pl.BlockSpec((1, 6, D_MODEL), _mod_index(TM)), whole(g1), whole(wcat),
                whole(wb), whole(ab), whole(gb), whole(w0), whole(a0), whole(kkw), whole(ka),
                whole(rk), whole(bd),
                pl.BlockSpec((TM, W_B), rope_idx), pl.BlockSpec((TM, W_B), rope_idx),
                pl.BlockSpec((TM, W_B), rope_idx)]
    out_specs = [tok(N_SCAN_KEYS * W_A)] + [tok(W_A)] * 3 + [heads] * 3 + [tok(W_C)] * 2
    out_shape = ([SDS((N_TOK, N_SCAN_KEYS * W_A), F32)] + [SDS((N_TOK, W_A), F32)] * 3
                 + [SDS((H_B, N_TOK, HEAD_DIM), F32)] * 3 + [SDS((N_TOK, W_C), F32)] * 2)
    return pl.pallas_call(
        _proj_kernel, grid=(nb,), in_specs=in_specs, out_specs=out_specs, out_shape=out_shape,
        compiler_params=_cparams("arbitrary"), name="proj",
    )(x, mod, g1, wcat, wb, ab, gb, w0, a0, kkw, ka, rk, bd, cos_t, sa_t, sb_t)


N_SCAN_ACC = 4


def _rwkv_step(state, keys, t, ops, v_ref, o_ref):
    nk, r, w, b, k = ops

    def row(a, j):
        return keys[t, a, pl.ds(j, 1), :]

    def total(parts):
        return (parts[0] + parts[1]) + (parts[2] + parts[3])

    parts = [None] * N_SCAN_ACC
    for j in range(HEAD_DIM):
        term = state[j] * row(nk, j)
        parts[j % N_SCAN_ACC] = term if parts[j % N_SCAN_ACC] is None else parts[j % N_SCAN_ACC] + term
    sa = total(parts)
    vt = v_ref[t]
    parts = [None] * N_SCAN_ACC
    for j in range(HEAD_DIM):
        sj = state[j] * row(w, j) + sa * row(b, j) + vt * row(k, j)
        state[j] = sj
        term = sj * row(r, j)
        parts[j % N_SCAN_ACC] = term if parts[j % N_SCAN_ACC] is None else parts[j % N_SCAN_ACC] + term
    o_ref[t] = total(parts)


def _rwkv_kernel(keys_f, vf, keys_b, vb, s0_ref, of_ref, ob_ref, sl_ref, s_scr, *, tc):
    i = pl.program_id(0)

    @pl.when(i == 0)
    def _():
        s_scr[...] = s0_ref[...]

    def step(tt, carry):
        _rwkv_step(s_scr.at[0], keys_f, tt, (0, 1, 2, 3, 4), vf, of_ref)
        _rwkv_step(s_scr.at[1], keys_b, tc - 1 - tt, (0, 1, 5, 6, 7), vb, ob_ref)
        return carry

    lax.fori_loop(0, tc, step, 0)

    @pl.when(i == pl.num_programs(0) - 1)
    def _():
        sl_ref[...] = s_scr[...]


def _rwkv_scan(keys, v, s0):
    t_len = keys.shape[0]
    tc = TC_SCAN
    nt = t_len // tc
    fwd = pl.BlockSpec((tc, HEAD_DIM, 128), lambda i: (i, 0, 0))
    bwd = pl.BlockSpec((tc, HEAD_DIM, 128), lambda i: (nt - 1 - i, 0, 0))
    kfwd = pl.BlockSpec((tc, N_SCAN_KEYS, HEAD_DIM, 128), lambda i: (i, 0, 0, 0))
    kbwd = pl.BlockSpec((tc, N_SCAN_KEYS, HEAD_DIM, 128), lambda i: (nt - 1 - i, 0, 0, 0))
    st = pl.BlockSpec((2, HEAD_DIM, HEAD_DIM, 128), lambda i: (0, 0, 0, 0))
    return pl.pallas_call(
        functools.partial(_rwkv_kernel, tc=tc),
        grid=(nt,),
        in_specs=[kfwd, fwd, kbwd, bwd, st],
        out_specs=[fwd, bwd, st],
        out_shape=[SDS((t_len, HEAD_DIM, 128), F32), SDS((t_len, HEAD_DIM, 128), F32),
                   SDS((2, HEAD_DIM, HEAD_DIM, 128), F32)],
        scratch_shapes=[pltpu.VMEM((2, HEAD_DIM, HEAD_DIM, 128), F32)],
        compiler_params=_cparams("arbitrary"), name="rwkv_scan",
    )(keys, v, keys, v, s0)


def _rwkv_fold_kernel(keys, v, s0_ref, o_ref, s_scr, *, tc):
    @pl.when(pl.program_id(0) == 0)
    def _():
        s_scr[...] = s0_ref[...]

    def step(tt, carry):
        _rwkv_step(s_scr, keys, tt, (0, 1, 2, 3, 4), v, o_ref)
        return carry

    lax.fori_loop(0, tc, step, 0)


def _rwkv_scan_folded(keys, v, s0):
    t_len = keys.shape[0]
    i_rows = v.shape[1]
    tc = TC_SCAN
    blk = pl.BlockSpec((tc, i_rows, 128), lambda i: (i, 0, 0))
    return pl.pallas_call(
        functools.partial(_rwkv_fold_kernel, tc=tc),
        grid=(t_len // tc,),
        in_specs=[pl.BlockSpec((tc, N_FOLD_KEYS, HEAD_DIM, 128), lambda i: (i, 0, 0, 0)), blk,
                  pl.BlockSpec((HEAD_DIM, i_rows, 128), lambda i: (0, 0, 0))],
        out_specs=blk,
        out_shape=SDS((t_len, i_rows, 128), F32),
        scratch_shapes=[pltpu.VMEM((HEAD_DIM, i_rows, 128), F32)],
        compiler_params=_cparams("arbitrary"), name="rwkv_scan_folded",
    )(keys, v, s0)


def _to_scan_keys(a, nb, t_len):
    a = a.reshape(nb, t_len, N_SCAN_KEYS, H_A, HEAD_DIM).transpose(1, 2, 4, 0, 3)
    return a.reshape(t_len, N_SCAN_KEYS, HEAD_DIM, nb * H_A)


def _to_scan_vals(a, nb, t_len):
    return a.reshape(nb, t_len, H_A, HEAD_DIM).transpose(1, 3, 0, 2).reshape(t_len, HEAD_DIM, nb * H_A)


def _from_scan_vals(o, nb, t_len):
    return o.reshape(t_len, HEAD_DIM, nb, H_A).transpose(2, 0, 3, 1).reshape(nb * t_len, W_A)


def _state_from_scan(s, nb):
    return s.reshape(2, HEAD_DIM, HEAD_DIM, nb, H_A).transpose(3, 0, 4, 2, 1)


def _fold_rep(nb):
    return 128 // (2 * nb * H_A)


def _fold_keys(a, nb, t_len):
    a = a.reshape(nb, t_len, N_SCAN_KEYS, H_A, HEAD_DIM)
    fwd = a[:, :, 0:N_FOLD_KEYS]
    bwd = jnp.flip(jnp.concatenate([a[:, :, 0:2], a[:, :, N_FOLD_KEYS:]], axis=2), axis=1)
    both = jnp.stack([fwd, bwd]).transpose(2, 3, 5, 0, 1, 4)
    rep = _fold_rep(nb)
    both = jnp.broadcast_to(both[:, :, :, :, None], (t_len, N_FOLD_KEYS, HEAD_DIM, 2, rep, nb, H_A))
    return both.reshape(t_len, N_FOLD_KEYS, HEAD_DIM, 128)


def _fold_vals(a, nb, t_len):
    rep = _fold_rep(nb)
    a = a.reshape(nb, t_len, H_A, rep, HEAD_DIM // rep)
    both = jnp.stack([a, jnp.flip(a, axis=1)])
    return both.transpose(2, 5, 0, 4, 1, 3).reshape(t_len, HEAD_DIM // rep, 128)


def _unfold_vals(o, nb, t_len):
    rep = _fold_rep(nb)
    o = o.reshape(t_len, HEAD_DIM // rep, 2, rep, nb, H_A).transpose(2, 4, 0, 5, 3, 1)
    return o[0].reshape(nb * t_len, W_A), jnp.flip(o[1], axis=1).reshape(nb * t_len, W_A)


def _fold_state(s, nb):
    rep = _fold_rep(nb)
    s = s.reshape(nb, 2, H_A, rep, HEAD_DIM // rep, HEAD_DIM).transpose(5, 4, 1, 3, 0, 2)
    return s.reshape(HEAD_DIM, HEAD_DIM // rep, 128)


def _attn_kernel(*refs, has_ctx, out_scale):
    if has_ctx:
        lam_ref, q_ref, k_ref, v_ref, ck_ref, cv_ref, g_ref, o_ref = refs
    else:
        lam_ref, q_ref, k_ref, v_ref, g_ref, o_ref = refs
    lam = lam_ref[0]
    q = q_ref[0] * (DK ** -0.5 * LOG2_E)
    lane = lax.broadcasted_iota(jnp.int32, q.shape, 1)
    qs = [jnp.where(lane < DK, q, 0.0).astype(BF16), jnp.where(lane >= DK, q, 0.0).astype(BF16)]
    kn = k_ref[0].astype(BF16)
    vn = v_ref[0].astype(BF16)
    if has_ctx:
        kc = ck_ref[0, 0, 0].astype(BF16)
        vc = cv_ref[0, 0, 0].astype(BF16)
    o = None
    for c in range(2):
        s_n = _dot_nt(qs[c], kn)
        m = jnp.max(s_n, axis=-1, keepdims=True)
        if has_ctx:
            s_c = _dot_nt(qs[c], kc)
            m = jnp.maximum(m, jnp.max(s_c, axis=-1, keepdims=True))
        e_n = jnp.exp2(s_n - m)
        z = jnp.sum(e_n, axis=-1, keepdims=True)
        o_c = jnp.dot(e_n.astype(BF16), vn, preferred_element_type=F32)
        if has_ctx:
            e_c = jnp.exp2(s_c - m)
            z = z + jnp.sum(e_c, axis=-1, keepdims=True)
            o_c = o_c + jnp.dot(e_c.astype(BF16), vc, preferred_element_type=F32)
        coef = (1.0 / z) if c == 0 else (-lam / z)
        o = o_c * coef if o is None else o + o_c * coef
    o = o * lax.rsqrt(jnp.mean(o * o, axis=-1, keepdims=True) + SUBLN_EPS) * g_ref[...]
    o_ref[0] = o * out_scale


def _attention(lam, q, k, v, g, lam_init, *, nb, t_len, row_off, layer=None, ck=None, cv=None):
    has_ctx = ck is not None
    nq = t_len // TQ
    qoff = row_off // TQ
    koff = row_off // t_len
    in_specs = [pl.BlockSpec(memory_space=pltpu.SMEM),
                pl.BlockSpec((1, TQ, HEAD_DIM), lambda b, h, i: (h, qoff + b * nq + i, 0)),
                pl.BlockSpec((1, t_len, HEAD_DIM), lambda b, h, i: (h, koff + b, 0)),
                pl.BlockSpec((1, t_len, HEAD_DIM), lambda b, h, i: (h, koff + b, 0))]
    args = [lam, q, k, v]
    if has_ctx:
        cspec = pl.BlockSpec((1, 1, 1, PAST_LEN, HEAD_DIM), lambda b, h, i: (b, layer, h, 0, 0))
        in_specs += [cspec, cspec]
        args += [ck, cv]
    in_specs.append(pl.BlockSpec((1, HEAD_DIM), lambda b, h, i: (0, 0)))
    args.append(g)
    return pl.pallas_call(
        functools.partial(_attn_kernel, has_ctx=has_ctx, out_scale=1.0 - lam_init),
        grid=(nb, H_B, nq),
        in_specs=in_specs,
        out_specs=pl.BlockSpec((1, TQ, HEAD_DIM), lambda b, h, i: (h, b * nq + i, 0)),
        out_shape=SDS((H_B, nb * t_len, HEAD_DIM), F32),
        compiler_params=_cparams("arbitrary", "arbitrary", "arbitrary"),
        name="attn_ctx" if has_ctx else "attn",
    )(*args)


def _lru_kernel(xc_ref, gc_ref, cw_ref, cb_ref, wa_ref, ba_ref, wx_ref, bx_ref, lam_ref, s0_ref,
                y_ref, hl_ref, *, t_len):
    x = xc_ref[...]
    row = lax.broadcasted_iota(jnp.int32, x.shape, 0)

    def shifted(a, off, fill):
        if off == 0:
            return a
        if off < 0:
            return jnp.where(row >= -off, pltpu.roll(a, -off, 0), fill)
        return jnp.where(row < t_len - off, pltpu.roll(a, t_len - off, 0), fill)

    xconv = cb_ref[...]
    for j in range(CONV_W):
        xconv = xconv + shifted(x, j - CONV_LEFT, 0.0) * cw_ref[j:j + 1, :]
    hsum = None
    for d in range(2):
        gr = jax.nn.sigmoid(jnp.dot(xconv, wa_ref[d], precision=HI, preferred_element_type=F32)
                            + ba_ref[d:d + 1, :])
        gi = jax.nn.sigmoid(jnp.dot(xconv, wx_ref[d], precision=HI, preferred_element_type=F32)
                            + bx_ref[d:d + 1, :])
        nl = -lam_ref[d:d + 1, :]
        softplus = jnp.maximum(nl, 0.0) + jnp.log1p(jnp.exp(-jnp.abs(nl)))
        log_a = -LRU_C * gr * softplus
        a = jnp.exp(log_a)
        b = jnp.sqrt(1.0 - jnp.exp(2.0 * log_a)) * (gi * xconv)
        dist = 1
        while dist < t_len:
            off = -dist if d == 0 else dist
            a_sh = shifted(a, off, 1.0)
            b_sh = shifted(b, off, 0.0)
            b = a * b_sh + b
            a = a * a_sh
            dist *= 2
        hs = b + a * s0_ref[0, d:d + 1, :]
        last = t_len - 1 if d == 0 else 0
        hl_ref[0, d:d + 1, :] = hs[last:last + 1, :]
        hsum = hs if hsum is None else hsum + hs
    y_ref[...] = hsum * _gelu(gc_ref[...])


def _lru(xc, gc, cw, cb, wa, ba, wx, bx, lam, s0, *, nb, t_len, row_off):
    off = row_off // t_len

    def whole(a):
        return pl.BlockSpec(a.shape, lambda b: (0,) * a.ndim)

    tok = pl.BlockSpec((t_len, W_C), lambda b: (off + b, 0))
    st = pl.BlockSpec((1, 2, W_C), lambda b: (b, 0, 0))
    return pl.pallas_call(
        functools.partial(_lru_kernel, t_len=t_len),
        grid=(nb,),
        in_specs=[tok, tok, whole(cw), whole(cb), whole(wa), whole(ba), whole(wx), whole(bx),
                  whole(lam), st],
        out_specs=[pl.BlockSpec((t_len, W_C), lambda b: (b, 0)), st],
        out_shape=[SDS((nb * t_len, W_C), F32), SDS((nb, 2, W_C), F32)],
        compiler_params=_cparams("arbitrary"), name="lru",
    )(xc, gc, cw, cb, wa, ba, wx, bx, lam, s0)


def _mid_kernel(x_ref, mod_ref, of_ref, ob_ref, bonus_ref, g_ref, at_ref, y_ref, lng_ref, bd_ref,
                wout_ref, g2_ref, wqh_ref, wql_ref, x1_o, h2_o, q_o):
    bd = bd_ref[...]
    o = of_ref[...] + ob_ref[...]
    inv_n = 1.0 / HEAD_DIM
    mu = jnp.dot(o, bd, precision=HI, preferred_element_type=F32) * inv_n
    xc = o - mu
    var = jnp.dot(xc * xc, bd, precision=HI, preferred_element_type=F32) * inv_n
    out_a = (xc * lax.rsqrt(var + GN_EPS) * lng_ref[...] + bonus_ref[...]) * g_ref[...]
    cat = jnp.concatenate([out_a] + [at_ref[hh] for hh in range(H_B)] + [y_ref[...]], axis=-1)
    ymix = jnp.dot(cat.astype(BF16), wout_ref[...], preferred_element_type=F32)
    x1 = x_ref[...] + mod_ref[0, 2:3, :] * ymix
    x1_o[...] = x1
    xn = x1 * lax.rsqrt(jnp.mean(x1 * x1, axis=-1, keepdims=True) + RMS_EPS) * g2_ref[...]
    h2 = xn * (1.0 + mod_ref[0, 4:5, :]) + mod_ref[0, 3:4, :]
    h2_hi = h2.astype(BF16)
    h2_lo = (h2 - h2_hi.astype(F32)).astype(BF16)
    h2_o[...] = h2_hi
    wq_hi = wqh_ref[...]
    q_o[...] = (jnp.dot(h2_hi, wq_hi, preferred_element_type=F32)
                + (jnp.dot(h2_hi, wql_ref[...], preferred_element_type=F32)
                   + jnp.dot(h2_lo, wq_hi, preferred_element_type=F32)))


def _mid(x, mod, o_f, o_b, bonus, g, att, y, lng, bd, wout, g2, wq_hi, wq_lo):
    nb = N_TOK // TM

    def whole(a):
        return pl.BlockSpec(a.shape, lambda i: (0,) * a.ndim)

    tok = lambda w: pl.BlockSpec((TM, w), lambda i: (i, 0))
    return pl.pallas_call(
        _mid_kernel, grid=(nb,),
        in_specs=[tok(D_MODEL), pl.BlockSpec((1, 6, D_MODEL), _mod_index(TM)), tok(W_A), tok(W_A),
                  tok(W_A), tok(W_A), pl.BlockSpec((H_B, TM, HEAD_DIM), lambda i: (0, i, 0)),
                  tok(W_C), whole(lng), whole(bd), whole(wout), whole(g2), whole(wq_hi), whole(wq_lo)],
        out_specs=[tok(D_MODEL), tok(D_MODEL), tok(PEER_HEADS * PEER_DQ)],
        out_shape=[SDS((N_TOK, D_MODEL), F32), SDS((N_TOK, D_MODEL), BF16),
                   SDS((N_TOK, PEER_HEADS * PEER_DQ), F32)],
        compiler_params=_cparams("arbitrary"), name="mid",
    )(x, mod, o_f, o_b, bonus, g, att, y, lng, bd, wout, g2, wq_hi, wq_lo)


def _topk_kernel(q_ref, keys_ref, r2_o, k1_o, e1_o, e2_o, t_scr, i_scr):
    tn = q_ref.shape[0]
    kio = lax.broadcasted_iota(jnp.int32, (PEER_N_KEYS, tn), 0).astype(F32)
    aio = lax.broadcasted_iota(jnp.int32, (PEER_TOPK, tn), 0).astype(F32)
    neg_inf = -jnp.inf
    scores = []
    rank2 = jnp.full((PEER_N_KEYS, tn), float(PEER_TOPK), F32)
    for c in range(2):
        half = PEER_DQ // 2
        st = _dot_nt(keys_ref[0, c], q_ref[:, c * half:(c + 1) * half], precision=HI)
        x = st
        for kth in range(PEER_TOPK):
            m = jnp.max(x, axis=0, keepdims=True)
            first = jnp.min(jnp.where(x == m, kio, float(PEER_N_KEYS)), axis=0, keepdims=True)
            hit = kio == first
            x = jnp.where(hit, neg_inf, x)
            t_scr[c, kth:kth + 1, :] = m
            if c == 0:
                i_scr[kth:kth + 1, :] = first
            else:
                rank2 = jnp.where(hit, float(kth), rank2)
        scores.append(st)
    t1 = t_scr[0]
    t2 = t_scr[1]
    m0 = t1[0:1, :] + t2[0:1, :]
    f = t1 + t2[0:1, :]
    cnt = jnp.zeros((PEER_TOPK, tn), F32)
    z = jnp.zeros((1, tn), F32)
    for _ in range(PEER_TOPK):
        m = jnp.max(f, axis=0, keepdims=True)
        first = jnp.min(jnp.where(f == m, aio, float(PEER_TOPK)), axis=0, keepdims=True)
        hit = aio == first
        z = z + jnp.exp(m - m0)
        cnt = jnp.where(hit, cnt + 1.0, cnt)
        ch = jnp.sum(jnp.where(hit, cnt, 0.0), axis=0, keepdims=True)
        nt2 = jnp.sum(jnp.where(aio == ch, t2, 0.0), axis=0, keepdims=True)
        newf = jnp.where(ch >= float(PEER_TOPK), neg_inf, t1 + nt2)
        f = jnp.where(hit, newf, f)
    quota = jnp.zeros((PEER_N_KEYS, tn), F32)
    for a in range(PEER_TOPK):
        quota = jnp.where(kio == i_scr[a:a + 1, :], cnt[a:a + 1, :], quota)
    r2_o[0] = rank2.astype(BF16)
    k1_o[0] = quota
    e1_o[0] = jnp.exp(scores[0] - t1[0:1, :]) * (1.0 / z)
    e2_o[0] = jnp.exp(scores[1] - t2[0:1, :]).astype(BF16)


def _peer_topk(q, keys):
    tn = TN_PEER
    shape = (PEER_HEADS, PEER_N_KEYS, N_TOK)
    ospec = pl.BlockSpec((1, PEER_N_KEYS, tn), lambda i, h: (h, 0, i))
    return pl.pallas_call(
        _topk_kernel, grid=(N_TOK // tn, PEER_HEADS),
        in_specs=[pl.BlockSpec((tn, PEER_DQ), lambda i, h: (i, h)),
                  pl.BlockSpec((1, 2, PEER_N_KEYS, PEER_DQ // 2), lambda i, h: (h, 0, 0, 0))],
        out_specs=[ospec] * 4,
        out_shape=[SDS(shape, BF16), SDS(shape, F32), SDS(shape, F32), SDS(shape, BF16)],
        scratch_shapes=[pltpu.VMEM((2, PEER_TOPK, tn), F32), pltpu.VMEM((PEER_TOPK, tn), F32)],
        compiler_params=_cparams("arbitrary", "arbitrary"), name="peer_topk",
    )(q, keys)


def _dense_kernel(h2_ref, u_ref, vt_ref, r2_ref, e2_ref, k1_ref, e1_ref, x1_ref, mod_ref, fg_ref,
                  o_ref, acc, act_scr, p_scr, r2_scr, e2_scr, *, final):
    j = pl.program_id(1)

    @pl.when(j == 0)
    def _():
        acc[...] = jnp.zeros_like(acc)
        for hh in range(PEER_HEADS):
            r2_scr[hh * PEER_N_KEYS:(hh + 1) * PEER_N_KEYS, :] = r2_ref[hh]
            e2_scr[hh * PEER_N_KEYS:(hh + 1) * PEER_N_KEYS, :] = e2_ref[hh]

    act_scr[...] = _gelu(_dot_nt(u_ref[...], h2_ref[...])).astype(BF16)
    tile = (BF16_SUBLANES, 128)
    for lt in range(h2_ref.shape[0] // 128):
        ls = slice(lt * 128, (lt + 1) * 128)
        for r in range(ROWS_PEER):
            quota = [jnp.broadcast_to(k1_ref[hh, r:r + 1, ls], tile).astype(BF16)
                     for hh in range(PEER_HEADS)]
            e1 = [jnp.broadcast_to(e1_ref[hh, r:r + 1, ls], tile).astype(BF16)
                  for hh in range(PEER_HEADS)]
            for blk in range(PEER_N_KEYS // BF16_SUBLANES):
                gate = None
                for hh in range(PEER_HEADS):
                    rs = slice(hh * PEER_N_KEYS + blk * BF16_SUBLANES,
                               hh * PEER_N_KEYS + (blk + 1) * BF16_SUBLANES)
                    term = jnp.where(r2_scr[rs, ls] < quota[hh], e2_scr[rs, ls] * e1[hh], 0.0)
                    gate = term if gate is None else gate + term
                es = slice(r * PEER_N_KEYS + blk * BF16_SUBLANES,
                           r * PEER_N_KEYS + (blk + 1) * BF16_SUBLANES)
                p_scr[es, ls] = gate * act_scr[es, ls]
    acc[...] += jnp.dot(vt_ref[0], p_scr[...], preferred_element_type=F32)

    @pl.when(j == pl.num_programs(1) - 1)
    def _():
        x2 = x1_ref[...] + mod_ref[0, 5:6, :] * acc[...].T
        if final:
            x2 = x2 * lax.rsqrt(jnp.mean(x2 * x2, axis=-1, keepdims=True) + RMS_EPS) * fg_ref[...]
        o_ref[...] = x2


def _peer_dense(h2, u, vt, r2, e2, k1, e1, x1, mod, fg, *, final):
    tn = TN_PEER
    ne = ROWS_PEER * PEER_N_KEYS
    per_tok = pl.BlockSpec((PEER_HEADS, PEER_N_KEYS, tn), lambda i, j: (0, 0, i))
    per_row = pl.BlockSpec((PEER_HEADS, ROWS_PEER, tn), lambda i, j: (0, j, i))
    mod_idx = _mod_index(tn)
    return pl.pallas_call(
        functools.partial(_dense_kernel, final=final),
        grid=(N_TOK // tn, N_EXPERTS // ne),
        in_specs=[pl.BlockSpec((tn, D_MODEL), lambda i, j: (i, 0)),
                  pl.BlockSpec((ne, D_MODEL), lambda i, j: (j, 0)),
                  pl.BlockSpec((1, D_MODEL, ne), lambda i, j: (j, 0, 0)),
                  per_tok, per_tok, per_row, per_row,
                  pl.BlockSpec((tn, D_MODEL), lambda i, j: (i, 0)),
                  pl.BlockSpec((1, 6, D_MODEL), lambda i, j: mod_idx(i)),
                  pl.BlockSpec((1, D_MODEL), lambda i, j: (0, 0))],
        out_specs=pl.BlockSpec((tn, D_MODEL), lambda i, j: (i, 0)),
        out_shape=SDS((N_TOK, D_MODEL), F32),
        scratch_shapes=[pltpu.VMEM((D_MODEL, tn), F32), pltpu.VMEM((ne, tn), BF16),
                        pltpu.VMEM((ne, tn), BF16),
                        pltpu.VMEM((PEER_HEADS * PEER_N_KEYS, tn), BF16),
                        pltpu.VMEM((PEER_HEADS * PEER_N_KEYS, tn), BF16)],
        compiler_params=_cparams("arbitrary", "arbitrary"), name="peer_dense",
    )(h2, u, vt, r2, e2, k1, e1, x1, mod, fg)


def _block_diag(blocks):
    g, n, m = blocks.shape
    eye = jnp.eye(g, dtype=blocks.dtype)
    return (eye[:, None, :, None] * blocks[:, :, None, :]).reshape(g * n, g * m)


def _rope_tables():
    rows = DEC_SEQ // GRID_W
    row = jnp.repeat(jnp.arange(rows), GRID_W).astype(F32)
    col = jnp.tile(jnp.arange(GRID_W), rows).astype(F32)
    inv = ROPE_BASE ** (-jnp.arange(ROPE_PAIRS, dtype=F32) / ROPE_PAIRS)
    ang = jnp.concatenate([row[:, None] * inv, col[:, None] * inv], axis=-1)
    lane = jnp.arange(W_B)
    e = (lane % HEAD_DIM) % DK
    pair, parity = e // 2, e % 2
    cos = jnp.cos(ang)[:, pair]
    sin = jnp.sin(ang)[:, pair]
    sa = jnp.where(parity == 0, -sin, 0.0)
    sb = jnp.where(parity == 1, sin, 0.0)
    ident = jnp.ones((TM, W_B), F32)
    zero = jnp.zeros((TM, W_B), F32)
    return (jnp.concatenate([cos, ident]), jnp.concatenate([sa, zero]), jnp.concatenate([sb, zero]))


def kernel(x_prompt, x_sample, c, cache_k, cache_v, state_rwkv, state_lru, c_ctx, mod_w, mod_b, norm1_g, norm2_g, w_in, w_out, rwkv_w0, rwkv_wA, rwkv_wB, rwkv_a0, rwkv_aA, rwkv_aB, rwkv_gA, rwkv_gB, rwkv_kk, rwkv_ka, rwkv_rk, rwkv_ln_g, diff_lq1, diff_lk1, diff_lq2, diff_lk2, diff_subln_g, lru_conv_w, lru_conv_b, lru_wa, lru_ba, lru_wx, lru_bx, lru_lambda, peer_wq, peer_keys, peer_u, peer_v, final_norm_g):
    x = jnp.concatenate([x_prompt.reshape(N_CTX, D_MODEL), x_sample.reshape(N_LAT, D_MODEL)])
    cond = jnp.concatenate([c_ctx[None], c, jnp.zeros((16 - 1 - DEC_BATCH, D_MODEL), F32)])
    cos_t, sa_t, sb_t = _rope_tables()
    head_id = jnp.arange(W_A) // HEAD_DIM
    bd = (head_id[:, None] == head_id[None, :]).astype(F32)
    zero_rwkv = jnp.zeros((2, HEAD_DIM, HEAD_DIM, BATCH * H_A), F32)
    zero_lru = jnp.zeros((BATCH, 2, W_C), F32)
    fg = final_norm_g[None]

    ks_list, vs_list, sr_list, sl_list = [], [], [], []
    for l in range(DEPTH):
        lam_init = 0.8 - 0.6 * math.exp(-0.3 * l)
        lam = (jnp.exp(jnp.sum(diff_lq1[l] * diff_lk1[l])) - jnp.exp(jnp.sum(diff_lq2[l] * diff_lk2[l]))
               + lam_init).reshape(1)
        mod = _adaln(cond, mod_w[l], mod_b[l])[:1 + DEC_BATCH].reshape(1 + DEC_BATCH, 6, D_MODEL)

        wcat = jnp.concatenate([w_in[l], rwkv_wA[l, 0], rwkv_wA[l, 1], rwkv_aA[l, 0], rwkv_aA[l, 1],
                                rwkv_gA[l]], axis=1).astype(BF16)
        (skeys, v, bonus, g, q_h, k_h, v_h, xc, gc) = _proj(
            x, mod, norm1_g[l][None], wcat,
            _block_diag(rwkv_wB[l]).astype(BF16), _block_diag(rwkv_aB[l]).astype(BF16),
            rwkv_gB[l].astype(BF16), rwkv_w0[l].reshape(1, 2 * W_A), rwkv_a0[l].reshape(1, 2 * W_A),
            rwkv_kk[l][None], rwkv_ka[l][None], rwkv_rk[l].reshape(1, W_A), bd, cos_t, sa_t, sb_t)

        of_s, ob_s, s_last = _rwkv_scan(_to_scan_keys(skeys[:N_CTX], BATCH, SEQ),
                                        _to_scan_vals(v[:N_CTX], BATCH, SEQ), zero_rwkv)
        sr_list.append(_state_from_scan(s_last, BATCH))
        o_lat = _rwkv_scan_folded(_fold_keys(skeys[N_CTX:], DEC_BATCH, DEC_SEQ),
                                  _fold_vals(v[N_CTX:], DEC_BATCH, DEC_SEQ),
                                  _fold_state(state_rwkv[:, l], DEC_BATCH))
        of_lat, ob_lat = _unfold_vals(o_lat, DEC_BATCH, DEC_SEQ)
        o_f = jnp.concatenate([_from_scan_vals(of_s, BATCH, SEQ), of_lat])
        o_b = jnp.concatenate([_from_scan_vals(ob_s, BATCH, SEQ), ob_lat])

        sub_g = diff_subln_g[l][None]
        att_p = _attention(lam, q_h, k_h, v_h, sub_g, lam_init, nb=BATCH, t_len=SEQ, row_off=0)
        att_s = _attention(lam, q_h, k_h, v_h, sub_g, lam_init, nb=DEC_BATCH, t_len=DEC_SEQ,
                           row_off=N_CTX, layer=l, ck=cache_k, cv=cache_v)
        att = jnp.concatenate([att_p, att_s], axis=1)
        ks_list.append(k_h[:, :N_CTX].reshape(H_B, BATCH, SEQ, HEAD_DIM).transpose(1, 0, 2, 3))
        vs_list.append(v_h[:, :N_CTX].reshape(H_B, BATCH, SEQ, HEAD_DIM).transpose(1, 0, 2, 3))

        lru_args = (lru_conv_w[l], lru_conv_b[l][None], jax.vmap(_block_diag)(lru_wa[l]), lru_ba[l],
                    jax.vmap(_block_diag)(lru_wx[l]), lru_bx[l], lru_lambda[l])
        y_p, hl_p = _lru(xc, gc, *lru_args, zero_lru, nb=BATCH, t_len=SEQ, row_off=0)
        y_s, _ = _lru(xc, gc, *lru_args, state_lru[:, l], nb=DEC_BATCH, t_len=DEC_SEQ, row_off=N_CTX)
        sl_list.append(hl_p)
        y = jnp.concatenate([y_p, y_s])

        wq_hi = peer_wq[l].astype(BF16)
        wq_lo = (peer_wq[l] - wq_hi.astype(F32)).astype(BF16)
        x1, h2, q = _mid(x, mod, o_f, o_b, bonus, g, att, y, rwkv_ln_g[l][None], bd,
                         w_out[l].astype(BF16), norm2_g[l][None], wq_hi, wq_lo)
        r2, k1, e1, e2 = _peer_topk(q, peer_keys[l])
        ne = ROWS_PEER * PEER_N_KEYS
        vt = peer_v[l].astype(BF16).reshape(N_EXPERTS // ne, ne, D_MODEL).transpose(0, 2, 1)
        x = _peer_dense(h2, peer_u[l].astype(BF16), vt, r2, e2, k1, e1, x1, mod, fg,
                        final=(l == DEPTH - 1))

    y_prompt = x[:N_CTX].reshape(BATCH, SEQ, D_MODEL)
    y_sample = x[N_CTX:].reshape(DEC_BATCH, DEC_SEQ, D_MODEL)
    return (y_prompt, y_sample, jnp.stack(ks_list, axis=1), jnp.stack(vs_list, axis=1),
            jnp.stack(sr_list, axis=1), jnp.stack(sl_list, axis=1))
```

```python
import functools
import math

import jax
import jax.numpy as jnp
from jax import lax
from jax.experimental import pallas as pl
from jax.experimental.pallas import tpu as pltpu

F32 = jnp.float32
BF16 = jnp.bfloat16
HI = lax.Precision.HIGHEST
SDS = jax.ShapeDtypeStruct

D_MODEL = 1024
BATCH = 32
SEQ = 256
DEPTH = 2
DEC_BATCH = 8
DEC_SEQ = 1024
PAST_LEN = 512
GRID_W = 64
HEAD_DIM = 64
W_A = D_MODEL // 4
W_B = D_MODEL // 2
W_C = D_MODEL - W_A - W_B
H_A = W_A // HEAD_DIM
H_B = W_B // HEAD_DIM
DK = HEAD_DIM // 2
ROPE_PAIRS = DK // 4
ROPE_BASE = 10000.0
DECAY_SCALE = 0.606531
LORA_W = 64
LORA_G = 128
CONV_W = 4
CONV_LEFT = 2
LRU_C = 8.0
LRU_BLOCKS = W_C // HEAD_DIM
PEER_HEADS = 8
PEER_N_KEYS = 128
PEER_TOPK = 16
PEER_DQ = 256
N_EXPERTS = PEER_N_KEYS ** 2
D_IN = 3 * W_A + 3 * W_B + 2 * W_C
D_CAT = D_IN + 4 * LORA_W + LORA_G
RMS_EPS = 1e-6
GN_EPS = 64e-5
SUBLN_EPS = 1e-5
INV_SQRT2 = 0.7071067811865476

N_CTX = BATCH * SEQ
N_LAT = DEC_BATCH * DEC_SEQ
N_TOK = N_CTX + N_LAT

VMEM_LIMIT_BYTES = 56 * 1024 * 1024
TM = 256
TQ = 256
TC_SCAN = 16
TN_PEER = 512
ROWS_PEER = 8
N_SCAN_KEYS = 8
BF16_SUBLANES = 16


def _cparams(*sem):
    return pltpu.CompilerParams(dimension_semantics=sem, vmem_limit_bytes=VMEM_LIMIT_BYTES)


def _mod_index(block_tokens):
    nbc = N_CTX // block_tokens
    per_seq = DEC_SEQ // block_tokens
    return lambda i: (jnp.where(i < nbc, 0, 1 + (i - nbc) // per_seq), 0, 0)


def _gelu(x):
    return 0.5 * x * (1.0 + lax.erf(x * INV_SQRT2))


def _dot_nt(a, b, precision=None):
    return lax.dot_general(a, b, (((1,), (1,)), ((), ())), precision=precision,
                           preferred_element_type=F32)


def _adaln_kernel(c_ref, w_ref, b_ref, o_ref):
    c = c_ref[...]
    s = c * jax.nn.sigmoid(c)
    o_ref[...] = jnp.dot(s, w_ref[...], precision=HI, preferred_element_type=F32) + b_ref[...]


def _adaln(cond, w, b):
    rows, n = cond.shape[0], w.shape[1]
    tn = 1536
    return pl.pallas_call(
        _adaln_kernel,
        grid=(n // tn,),
        in_specs=[pl.BlockSpec((rows, D_MODEL), lambda j: (0, 0)),
                  pl.BlockSpec((D_MODEL, tn), lambda j: (0, j)),
                  pl.BlockSpec((1, tn), lambda j: (0, j))],
        out_specs=pl.BlockSpec((rows, tn), lambda j: (0, j)),
        out_shape=SDS((rows, n), F32),
        compiler_params=_cparams("arbitrary"),
        name="adaln",
    )(cond, w, b[None])


def _proj_kernel(x_ref, mod_ref, g1_ref, wcat_ref, wb_ref, ab_ref, gb_ref, w0_ref, a0_ref,
                 kkw_ref, ka_ref, rk_ref, bd_ref, cos_ref, sa_ref, sb_ref,
                 keys_o, v_o, bonus_o, g_o, q_o, k_o, vv_o, xc_o, gc_o):
    x = x_ref[...]
    xn = x * lax.rsqrt(jnp.mean(x * x, axis=-1, keepdims=True) + RMS_EPS) * g1_ref[...]
    h = xn * (1.0 + mod_ref[0, 1:2, :]) + mod_ref[0, 0:1, :]
    p = jnp.dot(h.astype(BF16), wcat_ref[...], preferred_element_type=F32)

    r = p[:, 0:W_A]
    k = p[:, W_A:2 * W_A]
    v = p[:, 2 * W_A:3 * W_A]
    o = 3 * W_A
    q_b = p[:, o:o + W_B]
    k_b = p[:, o + W_B:o + 2 * W_B]
    v_b = p[:, o + 2 * W_B:o + 3 * W_B]
    o += 3 * W_B
    xc_o[...] = p[:, o:o + W_C]
    gc_o[...] = p[:, o + W_C:o + 2 * W_C]
    o = D_IN
    lw = jnp.tanh(p[:, o:o + 2 * LORA_W])
    la = p[:, o + 2 * LORA_W:o + 4 * LORA_W]
    lg = jax.nn.sigmoid(p[:, o + 4 * LORA_W:o + 4 * LORA_W + LORA_G])

    wpre = jnp.dot(lw.astype(BF16), wb_ref[...], preferred_element_type=F32) + w0_ref[...]
    apre = jnp.dot(la.astype(BF16), ab_ref[...], preferred_element_type=F32) + a0_ref[...]
    wdec = jnp.exp(-DECAY_SCALE * jax.nn.sigmoid(wpre))
    aicl = jax.nn.sigmoid(apre)
    g_o[...] = jnp.dot(lg.astype(BF16), gb_ref[...], preferred_element_type=F32)

    bd = bd_ref[...]
    kq = k * kkw_ref[...]
    ss = jnp.dot(kq * kq, bd, precision=HI, preferred_element_type=F32)
    kk = kq * lax.rsqrt(jnp.maximum(ss, 1e-12))
    keys_o[:, 0:W_A] = -kk
    keys_o[:, W_A:2 * W_A] = r
    v_o[...] = v
    ka = ka_ref[...]
    for d in range(2):
        a_d = aicl[:, d * W_A:(d + 1) * W_A]
        base = (2 + 3 * d) * W_A
        keys_o[:, base:base + W_A] = wdec[:, d * W_A:(d + 1) * W_A]
        keys_o[:, base + W_A:base + 2 * W_A] = kk * a_d
        keys_o[:, base + 2 * W_A:base + 3 * W_A] = k * (1.0 + (a_d - 1.0) * ka)
    bonus_o[...] = jnp.dot(r * k * rk_ref[...], bd, precision=HI, preferred_element_type=F32) * v

    cos, sa, sb = cos_ref[...], sa_ref[...], sb_ref[...]

    def rope(z):
        return z * cos + pltpu.roll(z, W_B - 1, 1) * sa + pltpu.roll(z, 1, 1) * sb

    q_r = rope(q_b)
    k_r = rope(k_b)
    for hh in range(H_B):
        sl = slice(hh * HEAD_DIM, (hh + 1) * HEAD_DIM)
        q_o[hh] = q_r[:, sl]
        k_o[hh] = k_r[:, sl]
        vv_o[hh] = v_b[:, sl]


def _proj(x, mod, g1, wcat, wb, ab, gb, w0, a0, kkw, ka, rk, bd, cos_t, sa_t, sb_t):
    nb = N_TOK // TM
    nbc = N_CTX // TM
    per_seq = DEC_SEQ // TM
    ident_blk = per_seq

    def rope_idx(i):
        return (jnp.where(i < nbc, ident_blk, (i - nbc) % per_seq), 0)

    def whole(a):
        return pl.BlockSpec(a.shape, lambda i: (0,) * a.ndim)

    tok = lambda w: pl.BlockSpec((TM, w), lambda i: (i, 0))
    heads = pl.BlockSpec((H_B, TM, HEAD_DIM), lambda i: (0, i, 0))
    in_specs = [tok(D_MODEL), pl.BlockSpec((1, 6, D_MODEL), _mod_index(TM)), whole(g1), whole(wcat),
                whole(wb), whole(ab), whole(gb), whole(w0), whole(a0), whole(kkw), whole(ka),
                whole(rk), whole(bd),
                pl.BlockSpec((TM, W_B), rope_idx), pl.BlockSpec((TM, W_B), rope_idx),
                pl.BlockSpec((TM, W_B), rope_idx)]
    out_specs = [tok(N_SCAN_KEYS * W_A)] + [tok(W_A)] * 3 + [heads] * 3 + [tok(W_C)] * 2
    out_shape = ([SDS((N_TOK, N_SCAN_KEYS * W_A), F32)] + [SDS((N_TOK, W_A), F32)] * 3
                 + [SDS((H_B, N_TOK, HEAD_DIM), F32)] * 3 + [SDS((N_TOK, W_C), F32)] * 2)
    return pl.pallas_call(
        _proj_kernel, grid=(nb,), in_specs=in_specs, out_specs=out_specs, out_shape=out_shape,
        compiler_params=_cparams("arbitrary"), name="proj",
    )(x, mod, g1, wcat, wb, ab, gb, w0, a0, kkw, ka, rk, bd, cos_t, sa_t, sb_t)


def _rwkv_kernel(keys_f, vf, keys_b, vb, s0_ref, of_ref, ob_ref, sl_ref, s_scr, *, tc):
    i = pl.program_id(0)

    @pl.when(i == 0)
    def _():
        s_scr[...] = s0_ref[...]

    nacc = 4

    def tree(parts):
        return (parts[0] + parts[1]) + (parts[2] + parts[3])

    def one_dir(d, t, keys, v, o):
        nk, r, w, b, k = 0, 1, 2 + 3 * d, 3 + 3 * d, 4 + 3 * d

        def row(a, j):
            return keys[t, a, pl.ds(j, 1), :]

        parts = [None] * nacc
        for j in range(HEAD_DIM):
            term = s_scr[d, j] * row(nk, j)
            parts[j % nacc] = term if parts[j % nacc] is None else parts[j % nacc] + term
        sa = tree(parts)
        vt = v[t]
        parts = [None] * nacc
        for j in range(HEAD_DIM):
            sj = s_scr[d, j] * row(w, j) + sa * row(b, j) + vt * row(k, j)
            s_scr[d, j] = sj
            term = sj * row(r, j)
            parts[j % nacc] = term if parts[j % nacc] is None else parts[j % nacc] + term
        o[t] = tree(parts)

    def step(tt, carry):
        one_dir(0, tt, keys_f, vf, of_ref)
        one_dir(1, tc - 1 - tt, keys_b, vb, ob_ref)
        return carry

    lax.fori_loop(0, tc, step, 0)

    @pl.when(i == pl.num_programs(0) - 1)
    def _():
        sl_ref[...] = s_scr[...]


def _rwkv_scan(keys, v, s0):
    t_len = keys.shape[0]
    i_rows = v.shape[1]
    tc = TC_SCAN
    nt = t_len // tc
    fwd = lambda rows: pl.BlockSpec((tc, rows, 128), lambda i: (i, 0, 0))
    bwd = lambda rows: pl.BlockSpec((tc, rows, 128), lambda i: (nt - 1 - i, 0, 0))
    kfwd = pl.BlockSpec((tc, N_SCAN_KEYS, HEAD_DIM, 128), lambda i: (i, 0, 0, 0))
    kbwd = pl.BlockSpec((tc, N_SCAN_KEYS, HEAD_DIM, 128), lambda i: (nt - 1 - i, 0, 0, 0))
    st = pl.BlockSpec((2, HEAD_DIM, i_rows, 128), lambda i: (0, 0, 0, 0))
    return pl.pallas_call(
        functools.partial(_rwkv_kernel, tc=tc),
        grid=(nt,),
        in_specs=[kfwd, fwd(i_rows), kbwd, bwd(i_rows), st],
        out_specs=[fwd(i_rows), bwd(i_rows), st],
        out_shape=[SDS((t_len, i_rows, 128), F32), SDS((t_len, i_rows, 128), F32),
                   SDS((2, HEAD_DIM, i_rows, 128), F32)],
        scratch_shapes=[pltpu.VMEM((2, HEAD_DIM, i_rows, 128), F32)],
        compiler_params=_cparams("arbitrary"), name="rwkv_scan",
    )(keys, v, keys, v, s0)


def _to_scan_keys(a, nb, t_len, i_hi):
    a = a.reshape(nb, t_len, N_SCAN_KEYS, H_A, HEAD_DIM).transpose(1, 2, 4, 0, 3)
    return jnp.tile(a.reshape(t_len, N_SCAN_KEYS, HEAD_DIM, nb * H_A), (1, 1, 1, i_hi))


def _to_scan_vals(a, nb, t_len, i_hi):
    i_lo = HEAD_DIM // i_hi
    a = a.reshape(nb, t_len, H_A, i_hi, i_lo).transpose(1, 4, 3, 0, 2)
    return a.reshape(t_len, i_lo, i_hi * nb * H_A)


def _from_scan_vals(o, nb, t_len, i_hi):
    i_lo = HEAD_DIM // i_hi
    o = o.reshape(t_len, i_lo, i_hi, nb, H_A).transpose(3, 0, 4, 2, 1)
    return o.reshape(nb * t_len, W_A)


def _state_to_scan(s, nb, i_hi):
    i_lo = HEAD_DIM // i_hi
    s = s.reshape(nb, 2, H_A, i_hi, i_lo, HEAD_DIM).transpose(1, 5, 4, 3, 0, 2)
    return s.reshape(2, HEAD_DIM, i_lo, i_hi * nb * H_A)


def _state_from_scan(s, nb, i_hi):
    i_lo = HEAD_DIM // i_hi
    s = s.reshape(2, HEAD_DIM, i_lo, i_hi, nb, H_A).transpose(4, 0, 5, 3, 2, 1)
    return s.reshape(nb, 2, H_A, HEAD_DIM, HEAD_DIM)


def _attn_kernel(*refs, has_ctx, out_scale):
    if has_ctx:
        lam_ref, q_ref, k_ref, v_ref, ck_ref, cv_ref, g_ref, o_ref = refs
    else:
        lam_ref, q_ref, k_ref, v_ref, g_ref, o_ref = refs
    lam = lam_ref[0]
    q = q_ref[0]
    lane = lax.broadcasted_iota(jnp.int32, q.shape, 1)
    scale = DK ** -0.5
    qs = [jnp.where(lane < DK, q, 0.0).astype(BF16), jnp.where(lane >= DK, q, 0.0).astype(BF16)]
    kn = k_ref[0].astype(BF16)
    vn = v_ref[0].astype(BF16)
    if has_ctx:
        kc = ck_ref[0, 0, 0].astype(BF16)
        vc = cv_ref[0, 0, 0].astype(BF16)
    p_new = None
    p_ctx = None
    for c in range(2):
        s_n = _dot_nt(qs[c], kn) * scale
        m = jnp.max(s_n, axis=-1, keepdims=True)
        if has_ctx:
            s_c = _dot_nt(qs[c], kc) * scale
            m = jnp.maximum(m, jnp.max(s_c, axis=-1, keepdims=True))
        e_n = jnp.exp(s_n - m)
        z = jnp.sum(e_n, axis=-1, keepdims=True)
        if has_ctx:
            e_c = jnp.exp(s_c - m)
            z = z + jnp.sum(e_c, axis=-1, keepdims=True)
        coef = (1.0 / z) if c == 0 else (-lam / z)
        p_new = e_n * coef if p_new is None else p_new + e_n * coef
        if has_ctx:
            p_ctx = e_c * coef if p_ctx is None else p_ctx + e_c * coef
    o = jnp.dot(p_new.astype(BF16), vn, preferred_element_type=F32)
    if has_ctx:
        o = o + jnp.dot(p_ctx.astype(BF16), vc, preferred_element_type=F32)
    o = o * lax.rsqrt(jnp.mean(o * o, axis=-1, keepdims=True) + SUBLN_EPS) * g_ref[...]
    o_ref[0] = o * out_scale


def _attention(lam, q, k, v, g, lam_init, *, nb, t_len, row_off, layer=None, ck=None, cv=None):
    has_ctx = ck is not None
    nq = t_len // TQ
    qoff = row_off // TQ
    koff = row_off // t_len
    in_specs = [pl.BlockSpec(memory_space=pltpu.SMEM),
                pl.BlockSpec((1, TQ, HEAD_DIM), lambda b, h, i: (h, qoff + b * nq + i, 0)),
                pl.BlockSpec((1, t_len, HEAD_DIM), lambda b, h, i: (h, koff + b, 0)),
                pl.BlockSpec((1, t_len, HEAD_DIM), lambda b, h, i: (h, koff + b, 0))]
    args = [lam, q, k, v]
    if has_ctx:
        cspec = pl.BlockSpec((1, 1, 1, PAST_LEN, HEAD_DIM), lambda b, h, i: (b, layer, h, 0, 0))
        in_specs += [cspec, cspec]
        args += [ck, cv]
    in_specs.append(pl.BlockSpec((1, HEAD_DIM), lambda b, h, i: (0, 0)))
    args.append(g)
    return pl.pallas_call(
        functools.partial(_attn_kernel, has_ctx=has_ctx, out_scale=1.0 - lam_init),
        grid=(nb, H_B, nq),
        in_specs=in_specs,
        out_specs=pl.BlockSpec((1, TQ, HEAD_DIM), lambda b, h, i: (h, b * nq + i, 0)),
        out_shape=SDS((H_B, nb * t_len, HEAD_DIM), F32),
        compiler_params=_cparams("arbitrary", "arbitrary", "arbitrary"),
        name="attn_ctx" if has_ctx else "attn",
    )(*args)


def _lru_kernel(xc_ref, gc_ref, cw_ref, cb_ref, wa_ref, ba_ref, wx_ref, bx_ref, lam_ref, s0_ref,
                y_ref, hl_ref, *, t_len):
    x = xc_ref[...]
    row = lax.broadcasted_iota(jnp.int32, x.shape, 0)

    def shifted(a, off, fill):
        if off == 0:
            return a
        if off < 0:
            return jnp.where(row >= -off, pltpu.roll(a, -off, 0), fill)
        return jnp.where(row < t_len - off, pltpu.roll(a, t_len - off, 0), fill)

    xconv = cb_ref[...]
    for j in range(CONV_W):
        xconv = xconv + shifted(x, j - CONV_LEFT, 0.0) * cw_ref[j:j + 1, :]
    hsum = None
    for d in range(2):
        gr = jax.nn.sigmoid(jnp.dot(xconv, wa_ref[d], precision=HI, preferred_element_type=F32)
                            + ba_ref[d:d + 1, :])
        gi = jax.nn.sigmoid(jnp.dot(xconv, wx_ref[d], precision=HI, preferred_element_type=F32)
                            + bx_ref[d:d + 1, :])
        nl = -lam_ref[d:d + 1, :]
        softplus = jnp.maximum(nl, 0.0) + jnp.log1p(jnp.exp(-jnp.abs(nl)))
        log_a = -LRU_C * gr * softplus
        a = jnp.exp(log_a)
        b = jnp.sqrt(1.0 - jnp.exp(2.0 * log_a)) * (gi * xconv)
        dist = 1
        while dist < t_len:
            off = -dist if d == 0 else dist
            a_sh = shifted(a, off, 1.0)
            b_sh = shifted(b, off, 0.0)
            b = a * b_sh + b
            a = a * a_sh
            dist *= 2
        hs = b + a * s0_ref[0, d:d + 1, :]
        last = t_len - 1 if d == 0 else 0
        hl_ref[0, d:d + 1, :] = hs[last:last + 1, :]
        hsum = hs if hsum is None else hsum + hs
    y_ref[...] = hsum * _gelu(gc_ref[...])


def _lru(xc, gc, cw, cb, wa, ba, wx, bx, lam, s0, *, nb, t_len, row_off):
    off = row_off // t_len

    def whole(a):
        return pl.BlockSpec(a.shape, lambda b: (0,) * a.ndim)

    tok = pl.BlockSpec((t_len, W_C), lambda b: (off + b, 0))
    st = pl.BlockSpec((1, 2, W_C), lambda b: (b, 0, 0))
    return pl.pallas_call(
        functools.partial(_lru_kernel, t_len=t_len),
        grid=(nb,),
        in_specs=[tok, tok, whole(cw), whole(cb), whole(wa), whole(ba), whole(wx), whole(bx),
                  whole(lam), st],
        out_specs=[pl.BlockSpec((t_len, W_C), lambda b: (b, 0)), st],
        out_shape=[SDS((nb * t_len, W_C), F32), SDS((nb, 2, W_C), F32)],
        compiler_params=_cparams("arbitrary"), name="lru",
    )(xc, gc, cw, cb, wa, ba, wx, bx, lam, s0)


def _mid_kernel(x_ref, mod_ref, of_ref, ob_ref, bonus_ref, g_ref, at_ref, y_ref, lng_ref, bd_ref,
                wout_ref, g2_ref, wqh_ref, wql_ref, x1_o, h2_o, q_o):
    bd = bd_ref[...]
    o = of_ref[...] + ob_ref[...]
    inv_n = 1.0 / HEAD_DIM
    mu = jnp.dot(o, bd, precision=HI, preferred_element_type=F32) * inv_n
    xc = o - mu
    var = jnp.dot(xc * xc, bd, precision=HI, preferred_element_type=F32) * inv_n
    out_a = (xc * lax.rsqrt(var + GN_EPS) * lng_ref[...] + bonus_ref[...]) * g_ref[...]
    cat = jnp.concatenate([out_a] + [at_ref[hh] for hh in range(H_B)] + [y_ref[...]], axis=-1)
    ymix = jnp.dot(cat.astype(BF16), wout_ref[...], preferred_element_type=F32)
    x1 = x_ref[...] + mod_ref[0, 2:3, :] * ymix
    x1_o[...] = x1
    xn = x1 * lax.rsqrt(jnp.mean(x1 * x1, axis=-1, keepdims=True) + RMS_EPS) * g2_ref[...]
    h2 = xn * (1.0 + mod_ref[0, 4:5, :]) + mod_ref[0, 3:4, :]
    h2_hi = h2.astype(BF16)
    h2_lo = (h2 - h2_hi.astype(F32)).astype(BF16)
    h2_o[...] = h2_hi
    wq_hi = wqh_ref[...]
    q_o[...] = (jnp.dot(h2_hi, wq_hi, preferred_element_type=F32)
                + (jnp.dot(h2_hi, wql_ref[...], preferred_element_type=F32)
                   + jnp.dot(h2_lo, wq_hi, preferred_element_type=F32)))


def _mid(x, mod, o_f, o_b, bonus, g, att, y, lng, bd, wout, g2, wq_hi, wq_lo):
    nb = N_TOK // TM

    def whole(a):
        return pl.BlockSpec(a.shape, lambda i: (0,) * a.ndim)

    tok = lambda w: pl.BlockSpec((TM, w), lambda i: (i, 0))
    return pl.pallas_call(
        _mid_kernel, grid=(nb,),
        in_specs=[tok(D_MODEL), pl.BlockSpec((1, 6, D_MODEL), _mod_index(TM)), tok(W_A), tok(W_A),
                  tok(W_A), tok(W_A), pl.BlockSpec((H_B, TM, HEAD_DIM), lambda i: (0, i, 0)),
                  tok(W_C), whole(lng), whole(bd), whole(wout), whole(g2), whole(wq_hi), whole(wq_lo)],
        out_specs=[tok(D_MODEL), tok(D_MODEL), tok(PEER_HEADS * PEER_DQ)],
        out_shape=[SDS((N_TOK, D_MODEL), F32), SDS((N_TOK, D_MODEL), BF16),
                   SDS((N_TOK, PEER_HEADS * PEER_DQ), F32)],
        compiler_params=_cparams("arbitrary"), name="mid",
    )(x, mod, o_f, o_b, bonus, g, att, y, lng, bd, wout, g2, wq_hi, wq_lo)


def _topk_kernel(q_ref, keys_ref, r2_o, k1_o, e1_o, e2_o, t_scr, rank_scr):
    tn = q_ref.shape[0]
    kio = lax.broadcasted_iota(jnp.int32, (PEER_N_KEYS, tn), 0).astype(F32)
    aio = lax.broadcasted_iota(jnp.int32, (PEER_TOPK, tn), 0).astype(F32)
    neg_inf = -jnp.inf
    half = PEER_DQ // 2
    scores = [_dot_nt(keys_ref[0, c], q_ref[:, c * half:(c + 1) * half], precision=HI)
              for c in range(2)]

    def extract(first_index_ties):
        for c in range(2):
            x = scores[c]
            rank = jnp.full((PEER_N_KEYS, tn), float(PEER_TOPK), F32)
            for kth in range(PEER_TOPK):
                m = jnp.max(x, axis=0, keepdims=True)
                hit = x == m
                if first_index_ties:
                    first = jnp.min(jnp.where(hit, kio, float(PEER_N_KEYS)), axis=0, keepdims=True)
                    hit = kio == first
                rank = jnp.where(hit, float(kth), rank)
                x = jnp.where(hit, neg_inf, x)
                t_scr[c, kth:kth + 1, :] = m
            rank_scr[c] = rank

    extract(False)
    ranked = (jnp.sum(jnp.where(rank_scr[0] < float(PEER_TOPK), 1.0, 0.0), axis=0, keepdims=True)
              + jnp.sum(jnp.where(rank_scr[1] < float(PEER_TOPK), 1.0, 0.0), axis=0, keepdims=True))

    @pl.when(jnp.max(ranked) > float(2 * PEER_TOPK))
    def _():
        extract(True)

    t1 = t_scr[0]
    t2 = t_scr[1]
    m0 = t1[0:1, :] + t2[0:1, :]
    f = t1 + t2[0:1, :]
    cnt = jnp.zeros((PEER_TOPK, tn), F32)
    z = jnp.zeros((1, tn), F32)
    for _ in range(PEER_TOPK):
        m = jnp.max(f, axis=0, keepdims=True)
        first = jnp.min(jnp.where(f == m, aio, float(PEER_TOPK)), axis=0, keepdims=True)
        hit = aio == first
        z = z + jnp.exp(m - m0)
        cnt = jnp.where(hit, cnt + 1.0, cnt)
        ch = jnp.sum(jnp.where(hit, cnt, 0.0), axis=0, keepdims=True)
        nt2 = jnp.sum(jnp.where(aio == ch, t2, 0.0), axis=0, keepdims=True)
        newf = jnp.where(ch >= float(PEER_TOPK), neg_inf, t1 + nt2)
        f = jnp.where(hit, newf, f)
    quota = jnp.zeros((PEER_N_KEYS, tn), F32)
    rank1 = rank_scr[0]
    for a in range(PEER_TOPK):
        quota = jnp.where(rank1 == float(a), cnt[a:a + 1, :], quota)
    r2_o[0] = rank_scr[1].astype(BF16)
    k1_o[0] = quota
    e1_o[0] = jnp.exp(scores[0] - t1[0:1, :]) * (1.0 / z)
    e2_o[0] = jnp.exp(scores[1] - t2[0:1, :]).astype(BF16)


def _peer_topk(q, keys):
    tn = TN_PEER
    shape = (PEER_HEADS, PEER_N_KEYS, N_TOK)
    ospec = pl.BlockSpec((1, PEER_N_KEYS, tn), lambda i, h: (h, 0, i))
    return pl.pallas_call(
        _topk_kernel, grid=(N_TOK // tn, PEER_HEADS),
        in_specs=[pl.BlockSpec((tn, PEER_DQ), lambda i, h: (i, h)),
                  pl.BlockSpec((1, 2, PEER_N_KEYS, PEER_DQ // 2), lambda i, h: (h, 0, 0, 0))],
        out_specs=[ospec] * 4,
        out_shape=[SDS(shape, BF16), SDS(shape, F32), SDS(shape, F32), SDS(shape, BF16)],
        scratch_shapes=[pltpu.VMEM((2, PEER_TOPK, tn), F32), pltpu.VMEM((2, PEER_N_KEYS, tn), F32)],
        compiler_params=_cparams("arbitrary", "arbitrary"), name="peer_topk",
    )(q, keys)


def _dense_kernel(h2_ref, u_ref, vt_ref, r2_ref, e2_ref, k1_ref, e1_ref, x1_ref, mod_ref, fg_ref,
                  o_ref, acc, act_scr, p_scr, *, final):
    j = pl.program_id(1)

    @pl.when(j == 0)
    def _():
        acc[...] = jnp.zeros_like(acc)

    act_scr[...] = _gelu(_dot_nt(u_ref[...], h2_ref[...])).astype(BF16)
    tile = (BF16_SUBLANES, 128)
    for lt in range(h2_ref.shape[0] // 128):
        ls = slice(lt * 128, (lt + 1) * 128)
        for r in range(ROWS_PEER):
            quota = [jnp.broadcast_to(k1_ref[hh, r:r + 1, ls], tile).astype(BF16)
                     for hh in range(PEER_HEADS)]
            e1 = [jnp.broadcast_to(e1_ref[hh, r:r + 1, ls], tile).astype(BF16)
                  for hh in range(PEER_HEADS)]
            for blk in range(PEER_N_KEYS // BF16_SUBLANES):
                rs = slice(blk * BF16_SUBLANES, (blk + 1) * BF16_SUBLANES)
                gate = None
                for hh in range(PEER_HEADS):
                    term = jnp.where(r2_ref[hh, rs, ls] < quota[hh], e2_ref[hh, rs, ls] * e1[hh], 0.0)
                    gate = term if gate is None else gate + term
                es = slice(r * PEER_N_KEYS + blk * BF16_SUBLANES,
                           r * PEER_N_KEYS + (blk + 1) * BF16_SUBLANES)
                p_scr[es, ls] = gate * act_scr[es, ls]
    acc[...] += jnp.dot(vt_ref[...], p_scr[...], preferred_element_type=F32)

    @pl.when(j == pl.num_programs(1) - 1)
    def _():
        x2 = x1_ref[...] + mod_ref[0, 5:6, :] * acc[...].T
        if final:
            x2 = x2 * lax.rsqrt(jnp.mean(x2 * x2, axis=-1, keepdims=True) + RMS_EPS) * fg_ref[...]
        o_ref[...] = x2


def _peer_dense(h2, u, vt, r2, e2, k1, e1, x1, mod, fg, *, final):
    tn = TN_PEER
    ne = ROWS_PEER * PEER_N_KEYS
    per_tok = pl.BlockSpec((PEER_HEADS, PEER_N_KEYS, tn), lambda i, j: (0, 0, i))
    per_row = pl.BlockSpec((PEER_HEADS, ROWS_PEER, tn), lambda i, j: (0, j, i))
    mod_idx = _mod_index(tn)
    return pl.pallas_call(
        functools.partial(_dense_kernel, final=final),
        grid=(N_TOK // tn, N_EXPERTS // ne),
        in_specs=[pl.BlockSpec((tn, D_MODEL), lambda i, j: (i, 0)),
                  pl.BlockSpec((ne, D_MODEL), lambda i, j: (j, 0)),
                  pl.BlockSpec((D_MODEL, ne), lambda i, j: (0, j)),
                  per_tok, per_tok, per_row, per_row,
                  pl.BlockSpec((tn, D_MODEL), lambda i, j: (i, 0)),
                  pl.BlockSpec((1, 6, D_MODEL), lambda i, j: mod_idx(i)),
                  pl.BlockSpec((1, D_MODEL), lambda i, j: (0, 0))],
        out_specs=pl.BlockSpec((tn, D_MODEL), lambda i, j: (i, 0)),
        out_shape=SDS((N_TOK, D_MODEL), F32),
        scratch_shapes=[pltpu.VMEM((D_MODEL, tn), F32), pltpu.VMEM((ne, tn), BF16),
                        pltpu.VMEM((ne, tn), BF16)],
        compiler_params=_cparams("arbitrary", "arbitrary"), name="peer_dense",
    )(h2, u, vt, r2, e2, k1, e1, x1, mod, fg)


def _block_diag(blocks):
    g, n, m = blocks.shape
    eye = jnp.eye(g, dtype=blocks.dtype)
    return (eye[:, None, :, None] * blocks[:, :, None, :]).reshape(g * n, g * m)


def _rope_tables():
    rows = DEC_SEQ // GRID_W
    row = jnp.repeat(jnp.arange(rows), GRID_W).astype(F32)
    col = jnp.tile(jnp.arange(GRID_W), rows).astype(F32)
    inv = ROPE_BASE ** (-jnp.arange(ROPE_PAIRS, dtype=F32) / ROPE_PAIRS)
    ang = jnp.concatenate([row[:, None] * inv, col[:, None] * inv], axis=-1)
    lane = jnp.arange(W_B)
    e = (lane % HEAD_DIM) % DK
    pair, parity = e // 2, e % 2
    cos = jnp.cos(ang)[:, pair]
    sin = jnp.sin(ang)[:, pair]
    sa = jnp.where(parity == 0, -sin, 0.0)
    sb = jnp.where(parity == 1, sin, 0.0)
    ident = jnp.ones((TM, W_B), F32)
    zero = jnp.zeros((TM, W_B), F32)
    return (jnp.concatenate([cos, ident]), jnp.concatenate([sa, zero]), jnp.concatenate([sb, zero]))


def kernel(x_prompt, x_sample, c, cache_k, cache_v, state_rwkv, state_lru, c_ctx, mod_w, mod_b, norm1_g, norm2_g, w_in, w_out, rwkv_w0, rwkv_wA, rwkv_wB, rwkv_a0, rwkv_aA, rwkv_aB, rwkv_gA, rwkv_gB, rwkv_kk, rwkv_ka, rwkv_rk, rwkv_ln_g, diff_lq1, diff_lk1, diff_lq2, diff_lk2, diff_subln_g, lru_conv_w, lru_conv_b, lru_wa, lru_ba, lru_wx, lru_bx, lru_lambda, peer_wq, peer_keys, peer_u, peer_v, final_norm_g):
    x = jnp.concatenate([x_prompt.reshape(N_CTX, D_MODEL), x_sample.reshape(N_LAT, D_MODEL)])
    cond = jnp.concatenate([c_ctx[None], c, jnp.zeros((16 - 1 - DEC_BATCH, D_MODEL), F32)])
    cos_t, sa_t, sb_t = _rope_tables()
    head_id = jnp.arange(W_A) // HEAD_DIM
    bd = (head_id[:, None] == head_id[None, :]).astype(F32)
    zero_rwkv = jnp.zeros((2, HEAD_DIM, HEAD_DIM, BATCH * H_A), F32)
    zero_lru = jnp.zeros((BATCH, 2, W_C), F32)
    i_hi_lat = 128 // (DEC_BATCH * H_A)
    fg = final_norm_g[None]

    ks_list, vs_list, sr_list, sl_list = [], [], [], []
    for l in range(DEPTH):
        lam_init = 0.8 - 0.6 * math.exp(-0.3 * l)
        lam = (jnp.exp(jnp.sum(diff_lq1[l] * diff_lk1[l])) - jnp.exp(jnp.sum(diff_lq2[l] * diff_lk2[l]))
               + lam_init).reshape(1)
        mod = _adaln(cond, mod_w[l], mod_b[l])[:1 + DEC_BATCH].reshape(1 + DEC_BATCH, 6, D_MODEL)

        wcat = jnp.concatenate([w_in[l], rwkv_wA[l, 0], rwkv_wA[l, 1], rwkv_aA[l, 0], rwkv_aA[l, 1],
                                rwkv_gA[l]], axis=1).astype(BF16)
        (skeys, v, bonus, g, q_h, k_h, v_h, xc, gc) = _proj(
            x, mod, norm1_g[l][None], wcat,
            _block_diag(rwkv_wB[l]).astype(BF16), _block_diag(rwkv_aB[l]).astype(BF16),
            rwkv_gB[l].astype(BF16), rwkv_w0[l].reshape(1, 2 * W_A), rwkv_a0[l].reshape(1, 2 * W_A),
            rwkv_kk[l][None], rwkv_ka[l][None], rwkv_rk[l].reshape(1, W_A), bd, cos_t, sa_t, sb_t)

        o_f, o_b = [], []
        for (lo, nb, t_len, i_hi, s0) in (
                (0, BATCH, SEQ, 1, zero_rwkv),
                (N_CTX, DEC_BATCH, DEC_SEQ, i_hi_lat, _state_to_scan(state_rwkv[:, l], DEC_BATCH, i_hi_lat))):
            sl = slice(lo, lo + nb * t_len)
            of_s, ob_s, s_last = _rwkv_scan(_to_scan_keys(skeys[sl], nb, t_len, i_hi),
                                            _to_scan_vals(v[sl], nb, t_len, i_hi), s0)
            o_f.append(_from_scan_vals(of_s, nb, t_len, i_hi))
            o_b.append(_from_scan_vals(ob_s, nb, t_len, i_hi))
            if lo == 0:
                sr_list.append(_state_from_scan(s_last, nb, i_hi))
        o_f = jnp.concatenate(o_f)
        o_b = jnp.concatenate(o_b)

        sub_g = diff_subln_g[l][None]
        att_p = _attention(lam, q_h, k_h, v_h, sub_g, lam_init, nb=BATCH, t_len=SEQ, row_off=0)
        att_s = _attention(lam, q_h, k_h, v_h, sub_g, lam_init, nb=DEC_BATCH, t_len=DEC_SEQ,
                           row_off=N_CTX, layer=l, ck=cache_k, cv=cache_v)
        att = jnp.concatenate([att_p, att_s], axis=1)
        ks_list.append(k_h[:, :N_CTX].reshape(H_B, BATCH, SEQ, HEAD_DIM).transpose(1, 0, 2, 3))
        vs_list.append(v_h[:, :N_CTX].reshape(H_B, BATCH, SEQ, HEAD_DIM).transpose(1, 0, 2, 3))

        lru_args = (lru_conv_w[l], lru_conv_b[l][None], jax.vmap(_block_diag)(lru_wa[l]), lru_ba[l],
                    jax.vmap(_block_diag)(lru_wx[l]), lru_bx[l], lru_lambda[l])
        y_p, hl_p = _lru(xc, gc, *lru_args, zero_lru, nb=BATCH, t_len=SEQ, row_off=0)
        y_s, _ = _lru(xc, gc, *lru_args, state_lru[:, l], nb=DEC_BATCH, t_len=DEC_SEQ, row_off=N_CTX)
        sl_list.append(hl_p)
        y = jnp.concatenate([y_p, y_s])

        wq_hi = peer_wq[l].astype(BF16)
        wq_lo = (peer_wq[l] - wq_hi.astype(F32)).astype(BF16)
        x1, h2, q = _mid(x, mod, o_f, o_b, bonus, g, att, y, rwkv_ln_g[l][None], bd,
                         w_out[l].astype(BF16), norm2_g[l][None], wq_hi, wq_lo)
        r2, k1, e1, e2 = _peer_topk(q, peer_keys[l])
        x = _peer_dense(h2, peer_u[l].astype(BF16), peer_v[l].T.astype(BF16), r2, e2, k1, e1, x1, mod, fg,
                        final=(l == DEPTH - 1))

    y_prompt = x[:N_CTX].reshape(BATCH, SEQ, D_MODEL)
    y_sample = x[N_CTX:].reshape(DEC_BATCH, DEC_SEQ, D_MODEL)
    return (y_prompt, y_sample, jnp.stack(ks_list, axis=1), jnp.stack(vs_list, axis=1),
            jnp.stack(sr_list, axis=1), jnp.stack(sl_list, axis=1))
```

```python
import functools
import math

import jax
import jax.numpy as jnp
from jax import lax
from jax.experimental import pallas as pl
from jax.experimental.pallas import tpu as pltpu

F32 = jnp.float32
BF16 = jnp.bfloat16
HI = lax.Precision.HIGHEST
SDS = jax.ShapeDtypeStruct

D_MODEL = 1024
BATCH = 32
SEQ = 256
DEPTH = 2
DEC_BATCH = 8
DEC_SEQ = 1024
PAST_LEN = 512
GRID_W = 64
HEAD_DIM = 64
W_A = D_MODEL // 4
W_B = D_MODEL // 2
W_C = D_MODEL - W_A - W_B
H_A = W_A // HEAD_DIM
H_B = W_B // HEAD_DIM
DK = HEAD_DIM // 2
ROPE_PAIRS = DK // 4
ROPE_BASE = 10000.0
DECAY_SCALE = 0.606531
LORA_W = 64
LORA_G = 128
CONV_W = 4
CONV_LEFT = 2
LRU_C = 8.0
LRU_BLOCKS = W_C // HEAD_DIM
PEER_HEADS = 8
PEER_N_KEYS = 128
PEER_TOPK = 16
PEER_DQ = 256
N_EXPERTS = PEER_N_KEYS ** 2
D_IN = 3 * W_A + 3 * W_B + 2 * W_C
D_CAT = D_IN + 4 * LORA_W + LORA_G
RMS_EPS = 1e-6
GN_EPS = 64e-5
SUBLN_EPS = 1e-5
INV_SQRT2 = 0.7071067811865476

N_CTX = BATCH * SEQ
N_LAT = DEC_BATCH * DEC_SEQ
N_TOK = N_CTX + N_LAT

VMEM_LIMIT_BYTES = 56 * 1024 * 1024
TM = 256
TQ = 256
TC_SCAN = 16
TN_PEER = 512
ROWS_PEER = 8
N_SCAN_KEYS = 8
BF16_SUBLANES = 16


def _cparams(*sem):
    return pltpu.CompilerParams(dimension_semantics=sem, vmem_limit_bytes=VMEM_LIMIT_BYTES)


def _mod_index(block_tokens):
    nbc = N_CTX // block_tokens
    per_seq = DEC_SEQ // block_tokens
    return lambda i: (jnp.where(i < nbc, 0, 1 + (i - nbc) // per_seq), 0, 0)


def _gelu(x):
    return 0.5 * x * (1.0 + lax.erf(x * INV_SQRT2))


def _dot_nt(a, b, precision=None):
    return lax.dot_general(a, b, (((1,), (1,)), ((), ())), precision=precision,
                           preferred_element_type=F32)


def _adaln_kernel(c_ref, w_ref, b_ref, o_ref):
    c = c_ref[...]
    s = c * jax.nn.sigmoid(c)
    o_ref[...] = jnp.dot(s, w_ref[...], precision=HI, preferred_element_type=F32) + b_ref[...]


def _adaln(cond, w, b):
    rows, n = cond.shape[0], w.shape[1]
    tn = 1536
    return pl.pallas_call(
        _adaln_kernel,
        grid=(n // tn,),
        in_specs=[pl.BlockSpec((rows, D_MODEL), lambda j: (0, 0)),
                  pl.BlockSpec((D_MODEL, tn), lambda j: (0, j)),
                  pl.BlockSpec((1, tn), lambda j: (0, j))],
        out_specs=pl.BlockSpec((rows, tn), lambda j: (0, j)),
        out_shape=SDS((rows, n), F32),
        compiler_params=_cparams("arbitrary"),
        name="adaln",
    )(cond, w, b[None])


def _proj_kernel(x_ref, mod_ref, g1_ref, wcat_ref, wb_ref, ab_ref, gb_ref, w0_ref, a0_ref,
                 kkw_ref, ka_ref, rk_ref, bd_ref, cos_ref, sa_ref, sb_ref,
                 keys_o, v_o, bonus_o, g_o, q_o, k_o, vv_o, xc_o, gc_o):
    x = x_ref[...]
    xn = x * lax.rsqrt(jnp.mean(x * x, axis=-1, keepdims=True) + RMS_EPS) * g1_ref[...]
    h = xn * (1.0 + mod_ref[0, 1:2, :]) + mod_ref[0, 0:1, :]
    p = jnp.dot(h.astype(BF16), wcat_ref[...], preferred_element_type=F32)

    r = p[:, 0:W_A]
    k = p[:, W_A:2 * W_A]
    v = p[:, 2 * W_A:3 * W_A]
    o = 3 * W_A
    q_b = p[:, o:o + W_B]
    k_b = p[:, o + W_B:o + 2 * W_B]
    v_b = p[:, o + 2 * W_B:o + 3 * W_B]
    o += 3 * W_B
    xc_o[...] = p[:, o:o + W_C]
    gc_o[...] = p[:, o + W_C:o + 2 * W_C]
    o = D_IN
    lw = jnp.tanh(p[:, o:o + 2 * LORA_W])
    la = p[:, o + 2 * LORA_W:o + 4 * LORA_W]
    lg = jax.nn.sigmoid(p[:, o + 4 * LORA_W:o + 4 * LORA_W + LORA_G])

    wpre = jnp.dot(lw.astype(BF16), wb_ref[...], preferred_element_type=F32) + w0_ref[...]
    apre = jnp.dot(la.astype(BF16), ab_ref[...], preferred_element_type=F32) + a0_ref[...]
    wdec = jnp.exp(-DECAY_SCALE * jax.nn.sigmoid(wpre))
    aicl = jax.nn.sigmoid(apre)
    g_o[...] = jnp.dot(lg.astype(BF16), gb_ref[...], preferred_element_type=F32)

    bd = bd_ref[...]
    kq = k * kkw_ref[...]
    ss = jnp.dot(kq * kq, bd, precision=HI, preferred_element_type=F32)
    kk = kq * lax.rsqrt(jnp.maximum(ss, 1e-12))
    keys_o[:, 0:W_A] = -kk
    keys_o[:, W_A:2 * W_A] = r
    v_o[...] = v
    ka = ka_ref[...]
    for d in range(2):
        a_d = aicl[:, d * W_A:(d + 1) * W_A]
        base = (2 + 3 * d) * W_A
        keys_o[:, base:base + W_A] = wdec[:, d * W_A:(d + 1) * W_A]
        keys_o[:, base + W_A:base + 2 * W_A] = kk * a_d
        keys_o[:, base + 2 * W_A:base + 3 * W_A] = k * (1.0 + (a_d - 1.0) * ka)
    bonus_o[...] = jnp.dot(r * k * rk_ref[...], bd, precision=HI, preferred_element_type=F32) * v

    cos, sa, sb = cos_ref[...], sa_ref[...], sb_ref[...]

    def rope(z):
        return z * cos + pltpu.roll(z, W_B - 1, 1) * sa + pltpu.roll(z, 1, 1) * sb

    q_r = rope(q_b)
    k_r = rope(k_b)
    for hh in range(H_B):
        sl = slice(hh * HEAD_DIM, (hh + 1) * HEAD_DIM)
        q_o[hh] = q_r[:, sl]
        k_o[hh] = k_r[:, sl]
        vv_o[hh] = v_b[:, sl]


def _proj(x, mod, g1, wcat, wb, ab, gb, w0, a0, kkw, ka, rk, bd, cos_t, sa_t, sb_t):
    nb = N_TOK // TM
    nbc = N_CTX // TM
    per_seq = DEC_SEQ // TM
    ident_blk = per_seq

    def rope_idx(i):
        return (jnp.where(i < nbc, ident_blk, (i - nbc) % per_seq), 0)

    def whole(a):
        return pl.BlockSpec(a.shape, lambda i: (0,) * a.ndim)

    tok = lambda w: pl.BlockSpec((TM, w), lambda i: (i, 0))
    heads = pl.BlockSpec((H_B, TM, HEAD_DIM), lambda i: (0, i, 0))
    in_specs = [tok(D_MODEL), pl.BlockSpec((1, 6, D_MODEL), _mod_index(TM)), whole(g1), whole(wcat),
                whole(wb), whole(ab), whole(gb), whole(w0), whole(a0), whole(kkw), whole(ka),
                whole(rk), whole(bd),
                pl.BlockSpec((TM, W_B), rope_idx), pl.BlockSpec((TM, W_B), rope_idx),
                pl.BlockSpec((TM, W_B), rope_idx)]
    out_specs = [tok(N_SCAN_KEYS * W_A)] + [tok(W_A)] * 3 + [heads] * 3 + [tok(W_C)] * 2
    out_shape = ([SDS((N_TOK, N_SCAN_KEYS * W_A), F32)] + [SDS((N_TOK, W_A), F32)] * 3
                 + [SDS((H_B, N_TOK, HEAD_DIM), F32)] * 3 + [SDS((N_TOK, W_C), F32)] * 2)
    return pl.pallas_call(
        _proj_kernel, grid=(nb,), in_specs=in_specs, out_specs=out_specs, out_shape=out_shape,
        compiler_params=_cparams("arbitrary"), name="proj",
    )(x, mod, g1, wcat, wb, ab, gb, w0, a0, kkw, ka, rk, bd, cos_t, sa_t, sb_t)


def _rwkv_kernel(keys_f, vf, keys_b, vb, s0_ref, of_ref, ob_ref, sl_ref, s_scr, *, tc):
    i = pl.program_id(0)

    @pl.when(i == 0)
    def _():
        s_scr[...] = s0_ref[...]

    nacc = 4

    def tree(parts):
        return (parts[0] + parts[1]) + (parts[2] + parts[3])

    def one_dir(d, t, keys, v, o):
        nk, r, w, b, k = 0, 1, 2 + 3 * d, 3 + 3 * d, 4 + 3 * d

        def row(a, j):
            return keys[t, a, pl.ds(j, 1), :]

        parts = [None] * nacc
        for j in range(HEAD_DIM):
            term = s_scr[d, j] * row(nk, j)
            parts[j % nacc] = term if parts[j % nacc] is None else parts[j % nacc] + term
        sa = tree(parts)
        vt = v[t]
        parts = [None] * nacc
        for j in range(HEAD_DIM):
            sj = s_scr[d, j] * row(w, j) + sa * row(b, j) + vt * row(k, j)
            s_scr[d, j] = sj
            term = sj * row(r, j)
            parts[j % nacc] = term if parts[j % nacc] is None else parts[j % nacc] + term
        o[t] = tree(parts)

    def step(tt, carry):
        one_dir(0, tt, keys_f, vf, of_ref)
        one_dir(1, tc - 1 - tt, keys_b, vb, ob_ref)
        return carry

    lax.fori_loop(0, tc, step, 0)

    @pl.when(i == pl.num_programs(0) - 1)
    def _():
        sl_ref[...] = s_scr[...]


def _rwkv_scan(keys, v, s0):
    t_len = keys.shape[0]
    i_rows = v.shape[1]
    tc = TC_SCAN
    nt = t_len // tc
    fwd = lambda rows: pl.BlockSpec((tc, rows, 128), lambda i: (i, 0, 0))
    bwd = lambda rows: pl.BlockSpec((tc, rows, 128), lambda i: (nt - 1 - i, 0, 0))
    kfwd = pl.BlockSpec((tc, N_SCAN_KEYS, HEAD_DIM, 128), lambda i: (i, 0, 0, 0))
    kbwd = pl.BlockSpec((tc, N_SCAN_KEYS, HEAD_DIM, 128), lambda i: (nt - 1 - i, 0, 0, 0))
    st = pl.BlockSpec((2, HEAD_DIM, i_rows, 128), lambda i: (0, 0, 0, 0))
    return pl.pallas_call(
        functools.partial(_rwkv_kernel, tc=tc),
        grid=(nt,),
        in_specs=[kfwd, fwd(i_rows), kbwd, bwd(i_rows), st],
        out_specs=[fwd(i_rows), bwd(i_rows), st],
        out_shape=[SDS((t_len, i_rows, 128), F32), SDS((t_len, i_rows, 128), F32),
                   SDS((2, HEAD_DIM, i_rows, 128), F32)],
        scratch_shapes=[pltpu.VMEM((2, HEAD_DIM, i_rows, 128), F32)],
        compiler_params=_cparams("arbitrary"), name="rwkv_scan",
    )(keys, v, keys, v, s0)


def _to_scan_keys(a, nb, t_len, i_hi):
    a = a.reshape(nb, t_len, N_SCAN_KEYS, H_A, HEAD_DIM).transpose(1, 2, 4, 0, 3)
    return jnp.tile(a.reshape(t_len, N_SCAN_KEYS, HEAD_DIM, nb * H_A), (1, 1, 1, i_hi))


def _to_scan_vals(a, nb, t_len, i_hi):
    i_lo = HEAD_DIM // i_hi
    a = a.reshape(nb, t_len, H_A, i_hi, i_lo).transpose(1, 4, 3, 0, 2)
    return a.reshape(t_len, i_lo, i_hi * nb * H_A)


def _from_scan_vals(o, nb, t_len, i_hi):
    i_lo = HEAD_DIM // i_hi
    o = o.reshape(t_len, i_lo, i_hi, nb, H_A).transpose(3, 0, 4, 2, 1)
    return o.reshape(nb * t_len, W_A)


def _state_to_scan(s, nb, i_hi):
    i_lo = HEAD_DIM // i_hi
    s = s.reshape(nb, 2, H_A, i_hi, i_lo, HEAD_DIM).transpose(1, 5, 4, 3, 0, 2)
    return s.reshape(2, HEAD_DIM, i_lo, i_hi * nb * H_A)


def _state_from_scan(s, nb, i_hi):
    i_lo = HEAD_DIM // i_hi
    s = s.reshape(2, HEAD_DIM, i_lo, i_hi, nb, H_A).transpose(4, 0, 5, 3, 2, 1)
    return s.reshape(nb, 2, H_A, HEAD_DIM, HEAD_DIM)


def _attn_kernel(*refs, has_ctx, out_scale):
    if has_ctx:
        lam_ref, q_ref, k_ref, v_ref, ck_ref, cv_ref, g_ref, o_ref = refs
    else:
        lam_ref, q_ref, k_ref, v_ref, g_ref, o_ref = refs
    for hh in range(q_ref.shape[0]):
        _attn_head(refs, hh, has_ctx, out_scale)


def _attn_head(refs, hh, has_ctx, out_scale):
    if has_ctx:
        lam_ref, q_ref, k_ref, v_ref, ck_ref, cv_ref, g_ref, o_ref = refs
    else:
        lam_ref, q_ref, k_ref, v_ref, g_ref, o_ref = refs
    lam = lam_ref[0]
    q = q_ref[hh]
    lane = lax.broadcasted_iota(jnp.int32, q.shape, 1)
    scale = DK ** -0.5
    qs = [jnp.where(lane < DK, q, 0.0).astype(BF16), jnp.where(lane >= DK, q, 0.0).astype(BF16)]
    kn = k_ref[hh].astype(BF16)
    vn = v_ref[hh].astype(BF16)
    if has_ctx:
        kc = ck_ref[0, 0, hh].astype(BF16)
        vc = cv_ref[0, 0, hh].astype(BF16)
    p_new = None
    p_ctx = None
    for c in range(2):
        s_n = _dot_nt(qs[c], kn) * scale
        m = jnp.max(s_n, axis=-1, keepdims=True)
        if has_ctx:
            s_c = _dot_nt(qs[c], kc) * scale
            m = jnp.maximum(m, jnp.max(s_c, axis=-1, keepdims=True))
        e_n = jnp.exp(s_n - m)
        z = jnp.sum(e_n, axis=-1, keepdims=True)
        if has_ctx:
            e_c = jnp.exp(s_c - m)
            z = z + jnp.sum(e_c, axis=-1, keepdims=True)
        coef = (1.0 / z) if c == 0 else (-lam / z)
        p_new = e_n * coef if p_new is None else p_new + e_n * coef
        if has_ctx:
            p_ctx = e_c * coef if p_ctx is None else p_ctx + e_c * coef
    o = jnp.dot(p_new.astype(BF16), vn, preferred_element_type=F32)
    if has_ctx:
        o = o + jnp.dot(p_ctx.astype(BF16), vc, preferred_element_type=F32)
    o = o * lax.rsqrt(jnp.mean(o * o, axis=-1, keepdims=True) + SUBLN_EPS) * g_ref[...]
    o_ref[hh] = o * out_scale


def _attention(lam, q, k, v, g, lam_init, *, nb, t_len, row_off, layer=None, ck=None, cv=None):
    has_ctx = ck is not None
    nq = t_len // TQ
    qoff = row_off // TQ
    koff = row_off // t_len
    hb = 1 if has_ctx else H_B
    in_specs = [pl.BlockSpec(memory_space=pltpu.SMEM),
                pl.BlockSpec((hb, TQ, HEAD_DIM), lambda b, h, i: (h, qoff + b * nq + i, 0)),
                pl.BlockSpec((hb, t_len, HEAD_DIM), lambda b, h, i: (h, koff + b, 0)),
                pl.BlockSpec((hb, t_len, HEAD_DIM), lambda b, h, i: (h, koff + b, 0))]
    args = [lam, q, k, v]
    if has_ctx:
        cspec = pl.BlockSpec((1, 1, 1, PAST_LEN, HEAD_DIM), lambda b, h, i: (b, layer, h, 0, 0))
        in_specs += [cspec, cspec]
        args += [ck, cv]
    in_specs.append(pl.BlockSpec((1, HEAD_DIM), lambda b, h, i: (0, 0)))
    args.append(g)
    return pl.pallas_call(
        functools.partial(_attn_kernel, has_ctx=has_ctx, out_scale=1.0 - lam_init),
        grid=(nb, H_B // hb, nq),
        in_specs=in_specs,
        out_specs=pl.BlockSpec((hb, TQ, HEAD_DIM), lambda b, h, i: (h, b * nq + i, 0)),
        out_shape=SDS((H_B, nb * t_len, HEAD_DIM), F32),
        compiler_params=_cparams("arbitrary", "arbitrary", "arbitrary"),
        name="attn_ctx" if has_ctx else "attn",
    )(*args)


def _lru_kernel(xc_ref, gc_ref, cw_ref, cb_ref, wa_ref, ba_ref, wx_ref, bx_ref, lam_ref, s0_ref,
                y_ref, hl_ref, *, t_len):
    x = xc_ref[...]
    row = lax.broadcasted_iota(jnp.int32, x.shape, 0)

    def shifted(a, off, fill):
        if off == 0:
            return a
        if off < 0:
            return jnp.where(row >= -off, pltpu.roll(a, -off, 0), fill)
        return jnp.where(row < t_len - off, pltpu.roll(a, t_len - off, 0), fill)

    xconv = cb_ref[...]
    for j in range(CONV_W):
        xconv = xconv + shifted(x, j - CONV_LEFT, 0.0) * cw_ref[j:j + 1, :]
    hsum = None
    for d in range(2):
        gr = jax.nn.sigmoid(jnp.dot(xconv, wa_ref[d], precision=HI, preferred_element_type=F32)
                            + ba_ref[d:d + 1, :])
        gi = jax.nn.sigmoid(jnp.dot(xconv, wx_ref[d], precision=HI, preferred_element_type=F32)
                            + bx_ref[d:d + 1, :])
        nl = -lam_ref[d:d + 1, :]
        softplus = jnp.maximum(nl, 0.0) + jnp.log1p(jnp.exp(-jnp.abs(nl)))
        log_a = -LRU_C * gr * softplus
        a = jnp.exp(log_a)
        b = jnp.sqrt(1.0 - jnp.exp(2.0 * log_a)) * (gi * xconv)
        dist = 1
        while dist < t_len:
            off = -dist if d == 0 else dist
            a_sh = shifted(a, off, 1.0)
            b_sh = shifted(b, off, 0.0)
            b = a * b_sh + b
            a = a * a_sh
            dist *= 2
        hs = b + a * s0_ref[0, d:d + 1, :]
        last = t_len - 1 if d == 0 else 0
        hl_ref[0, d:d + 1, :] = hs[last:last + 1, :]
        hsum = hs if hsum is None else hsum + hs
    y_ref[...] = hsum * _gelu(gc_ref[...])


def _lru(xc, gc, cw, cb, wa, ba, wx, bx, lam, s0, *, nb, t_len, row_off):
    off = row_off // t_len

    def whole(a):
        return pl.BlockSpec(a.shape, lambda b: (0,) * a.ndim)

    tok = pl.BlockSpec((t_len, W_C), lambda b: (off + b, 0))
    st = pl.BlockSpec((1, 2, W_C), lambda b: (b, 0, 0))
    return pl.pallas_call(
        functools.partial(_lru_kernel, t_len=t_len),
        grid=(nb,),
        in_specs=[tok, tok, whole(cw), whole(cb), whole(wa), whole(ba), whole(wx), whole(bx),
                  whole(lam), st],
        out_specs=[pl.BlockSpec((t_len, W_C), lambda b: (b, 0)), st],
        out_shape=[SDS((nb * t_len, W_C), F32), SDS((nb, 2, W_C), F32)],
        compiler_params=_cparams("arbitrary"), name="lru",
    )(xc, gc, cw, cb, wa, ba, wx, bx, lam, s0)


def _mid_kernel(x_ref, mod_ref, of_ref, ob_ref, bonus_ref, g_ref, at_ref, y_ref, lng_ref, bd_ref,
                wout_ref, g2_ref, wqh_ref, wql_ref, x1_o, h2_o, q_o):
    bd = bd_ref[...]
    o = of_ref[...] + ob_ref[...]
    inv_n = 1.0 / HEAD_DIM
    mu = jnp.dot(o, bd, precision=HI, preferred_element_type=F32) * inv_n
    xc = o - mu
    var = jnp.dot(xc * xc, bd, precision=HI, preferred_element_type=F32) * inv_n
    out_a = (xc * lax.rsqrt(var + GN_EPS) * lng_ref[...] + bonus_ref[...]) * g_ref[...]
    cat = jnp.concatenate([out_a] + [at_ref[hh] for hh in range(H_B)] + [y_ref[...]], axis=-1)
    ymix = jnp.dot(cat.astype(BF16), wout_ref[...], preferred_element_type=F32)
    x1 = x_ref[...] + mod_ref[0, 2:3, :] * ymix
    x1_o[...] = x1
    xn = x1 * lax.rsqrt(jnp.mean(x1 * x1, axis=-1, keepdims=True) + RMS_EPS) * g2_ref[...]
    h2 = xn * (1.0 + mod_ref[0, 4:5, :]) + mod_ref[0, 3:4, :]
    h2_hi = h2.astype(BF16)
    h2_lo = (h2 - h2_hi.astype(F32)).astype(BF16)
    h2_o[...] = h2_hi
    wq_hi = wqh_ref[...]
    q_o[...] = (jnp.dot(h2_hi, wq_hi, preferred_element_type=F32)
                + (jnp.dot(h2_hi, wql_ref[...], preferred_element_type=F32)
                   + jnp.dot(h2_lo, wq_hi, preferred_element_type=F32)))


def _mid(x, mod, o_f, o_b, bonus, g, att, y, lng, bd, wout, g2, wq_hi, wq_lo):
    nb = N_TOK // TM

    def whole(a):
        return pl.BlockSpec(a.shape, lambda i: (0,) * a.ndim)

    tok = lambda w: pl.BlockSpec((TM, w), lambda i: (i, 0))
    return pl.pallas_call(
        _mid_kernel, grid=(nb,),
        in_specs=[tok(D_MODEL), pl.BlockSpec((1, 6, D_MODEL), _mod_index(TM)), tok(W_A), tok(W_A),
                  tok(W_A), tok(W_A), pl.BlockSpec((H_B, TM, HEAD_DIM), lambda i: (0, i, 0)),
                  tok(W_C), whole(lng), whole(bd), whole(wout), whole(g2), whole(wq_hi), whole(wq_lo)],
        out_specs=[tok(D_MODEL), tok(D_MODEL), tok(PEER_HEADS * PEER_DQ)],
        out_shape=[SDS((N_TOK, D_MODEL), F32), SDS((N_TOK, D_MODEL), BF16),
                   SDS((N_TOK, PEER_HEADS * PEER_DQ), F32)],
        compiler_params=_cparams("arbitrary"), name="mid",
    )(x, mod, o_f, o_b, bonus, g, att, y, lng, bd, wout, g2, wq_hi, wq_lo)


def _topk_kernel(q_ref, keys_ref, r2_o, k1_o, e1_o, e2_o, t_scr, rank_scr):
    tn = q_ref.shape[0]
    kio = lax.broadcasted_iota(jnp.int32, (PEER_N_KEYS, tn), 0).astype(F32)
    aio = lax.broadcasted_iota(jnp.int32, (PEER_TOPK, tn), 0).astype(F32)
    neg_inf = -jnp.inf
    half = PEER_DQ // 2
    scores = [_dot_nt(keys_ref[0, c], q_ref[:, c * half:(c + 1) * half], precision=HI)
              for c in range(2)]

    def extract(first_index_ties):
        for c in range(2):
            x = scores[c]
            rank = jnp.full((PEER_N_KEYS, tn), float(PEER_TOPK), F32)
            for kth in range(PEER_TOPK):
                m = jnp.max(x, axis=0, keepdims=True)
                hit = x == m
                if first_index_ties:
                    first = jnp.min(jnp.where(hit, kio, float(PEER_N_KEYS)), axis=0, keepdims=True)
                    hit = kio == first
                rank = jnp.where(hit, float(kth), rank)
                x = jnp.where(hit, neg_inf, x)
                t_scr[c, kth:kth + 1, :] = m
            rank_scr[c] = rank

    extract(False)
    ranked = (jnp.sum(jnp.where(rank_scr[0] < float(PEER_TOPK), 1.0, 0.0), axis=0, keepdims=True)
              + jnp.sum(jnp.where(rank_scr[1] < float(PEER_TOPK), 1.0, 0.0), axis=0, keepdims=True))

    @pl.when(jnp.max(ranked) > float(2 * PEER_TOPK))
    def _():
        extract(True)

    t1 = t_scr[0]
    t2 = t_scr[1]
    m0 = t1[0:1, :] + t2[0:1, :]
    f = t1 + t2[0:1, :]
    cnt = jnp.zeros((PEER_TOPK, tn), F32)
    z = jnp.zeros((1, tn), F32)
    for _ in range(PEER_TOPK):
        m = jnp.max(f, axis=0, keepdims=True)
        first = jnp.min(jnp.where(f == m, aio, float(PEER_TOPK)), axis=0, keepdims=True)
        hit = aio == first
        z = z + jnp.exp(m - m0)
        cnt = jnp.where(hit, cnt + 1.0, cnt)
        ch = jnp.sum(jnp.where(hit, cnt, 0.0), axis=0, keepdims=True)
        nt2 = jnp.sum(jnp.where(aio == ch, t2, 0.0), axis=0, keepdims=True)
        newf = jnp.where(ch >= float(PEER_TOPK), neg_inf, t1 + nt2)
        f = jnp.where(hit, newf, f)
    quota = jnp.zeros((PEER_N_KEYS, tn), F32)
    rank1 = rank_scr[0]
    for a in range(PEER_TOPK):
        quota = jnp.where(rank1 == float(a), cnt[a:a + 1, :], quota)
    r2_o[0] = rank_scr[1].astype(BF16)
    k1_o[0] = quota
    e1_o[0] = jnp.exp(scores[0] - t1[0:1, :]) * (1.0 / z)
    e2_o[0] = jnp.exp(scores[1] - t2[0:1, :]).astype(BF16)


def _peer_topk(q, keys):
    tn = TN_PEER
    shape = (PEER_HEADS, PEER_N_KEYS, N_TOK)
    ospec = pl.BlockSpec((1, PEER_N_KEYS, tn), lambda i, h: (h, 0, i))
    return pl.pallas_call(
        _topk_kernel, grid=(N_TOK // tn, PEER_HEADS),
        in_specs=[pl.BlockSpec((tn, PEER_DQ), lambda i, h: (i, h)),
                  pl.BlockSpec((1, 2, PEER_N_KEYS, PEER_DQ // 2), lambda i, h: (h, 0, 0, 0))],
        out_specs=[ospec] * 4,
        out_shape=[SDS(shape, BF16), SDS(shape, F32), SDS(shape, F32), SDS(shape, BF16)],
        scratch_shapes=[pltpu.VMEM((2, PEER_TOPK, tn), F32), pltpu.VMEM((2, PEER_N_KEYS, tn), F32)],
        compiler_params=_cparams("arbitrary", "arbitrary"), name="peer_topk",
    )(q, keys)


def _dense_kernel(h2_ref, u_ref, vt_ref, r2_ref, e2_ref, k1_ref, e1_ref, x1_ref, mod_ref, fg_ref,
                  o_ref, acc, act_scr, p_scr, *, final):
    j = pl.program_id(1)

    @pl.when(j == 0)
    def _():
        acc[...] = jnp.zeros_like(acc)

    act_scr[...] = _gelu(_dot_nt(u_ref[...], h2_ref[...])).astype(BF16)
    tile = (BF16_SUBLANES, 128)
    for lt in range(h2_ref.shape[0] // 128):
        ls = slice(lt * 128, (lt + 1) * 128)
        for r in range(ROWS_PEER):
            quota = [jnp.broadcast_to(k1_ref[hh, r:r + 1, ls], tile).astype(BF16)
                     for hh in range(PEER_HEADS)]
            e1 = [jnp.broadcast_to(e1_ref[hh, r:r + 1, ls], tile).astype(BF16)
                  for hh in range(PEER_HEADS)]
            for blk in range(PEER_N_KEYS // BF16_SUBLANES):
                rs = slice(blk * BF16_SUBLANES, (blk + 1) * BF16_SUBLANES)
                gate = None
                for hh in range(PEER_HEADS):
                    term = jnp.where(r2_ref[hh, rs, ls] < quota[hh], e2_ref[hh, rs, ls] * e1[hh], 0.0)
                    gate = term if gate is None else gate + term
                es = slice(r * PEER_N_KEYS + blk * BF16_SUBLANES,
                           r * PEER_N_KEYS + (blk + 1) * BF16_SUBLANES)
                p_scr[es, ls] = gate * act_scr[es, ls]
    acc[...] += jnp.dot(vt_ref[...], p_scr[...], preferred_element_type=F32)

    @pl.when(j == pl.num_programs(1) - 1)
    def _():
        x2 = x1_ref[...] + mod_ref[0, 5:6, :] * acc[...].T
        if final:
            x2 = x2 * lax.rsqrt(jnp.mean(x2 * x2, axis=-1, keepdims=True) + RMS_EPS) * fg_ref[...]
        o_ref[...] = x2


def _peer_dense(h2, u, vt, r2, e2, k1, e1, x1, mod, fg, *, final):
    tn = TN_PEER
    ne = ROWS_PEER * PEER_N_KEYS
    per_tok = pl.BlockSpec((PEER_HEADS, PEER_N_KEYS, tn), lambda i, j: (0, 0, i))
    per_row = pl.BlockSpec((PEER_HEADS, ROWS_PEER, tn), lambda i, j: (0, j, i))
    mod_idx = _mod_index(tn)
    return pl.pallas_call(
        functools.partial(_dense_kernel, final=final),
        grid=(N_TOK // tn, N_EXPERTS // ne),
        in_specs=[pl.BlockSpec((tn, D_MODEL), lambda i, j: (i, 0)),
                  pl.BlockSpec((ne, D_MODEL), lambda i, j: (j, 0)),
                  pl.BlockSpec((D_MODEL, ne), lambda i, j: (0, j)),
                  per_tok, per_tok, per_row, per_row,
                  pl.BlockSpec((tn, D_MODEL), lambda i, j: (i, 0)),
                  pl.BlockSpec((1, 6, D_MODEL), lambda i, j: mod_idx(i)),
                  pl.BlockSpec((1, D_MODEL), lambda i, j: (0, 0))],
        out_specs=pl.BlockSpec((tn, D_MODEL), lambda i, j: (i, 0)),
        out_shape=SDS((N_TOK, D_MODEL), F32),
        scratch_shapes=[pltpu.VMEM((D_MODEL, tn), F32), pltpu.VMEM((ne, tn), BF16),
                        pltpu.VMEM((ne, tn), BF16)],
        compiler_params=_cparams("arbitrary", "arbitrary"), name="peer_dense",
    )(h2, u, vt, r2, e2, k1, e1, x1, mod, fg)


def _block_diag(blocks):
    g, n, m = blocks.shape
    eye = jnp.eye(g, dtype=blocks.dtype)
    return (eye[:, None, :, None] * blocks[:, :, None, :]).reshape(g * n, g * m)


def _rope_tables():
    rows = DEC_SEQ // GRID_W
    row = jnp.repeat(jnp.arange(rows), GRID_W).astype(F32)
    col = jnp.tile(jnp.arange(GRID_W), rows).astype(F32)
    inv = ROPE_BASE ** (-jnp.arange(ROPE_PAIRS, dtype=F32) / ROPE_PAIRS)
    ang = jnp.concatenate([row[:, None] * inv, col[:, None] * inv], axis=-1)
    lane = jnp.arange(W_B)
    e = (lane % HEAD_DIM) % DK
    pair, parity = e // 2, e % 2
    cos = jnp.cos(ang)[:, pair]
    sin = jnp.sin(ang)[:, pair]
    sa = jnp.where(parity == 0, -sin, 0.0)
    sb = jnp.where(parity == 1, sin, 0.0)
    ident = jnp.ones((TM, W_B), F32)
    zero = jnp.zeros((TM, W_B), F32)
    return (jnp.concatenate([cos, ident]), jnp.concatenate([sa, zero]), jnp.concatenate([sb, zero]))


def kernel(x_prompt, x_sample, c, cache_k, cache_v, state_rwkv, state_lru, c_ctx, mod_w, mod_b, norm1_g, norm2_g, w_in, w_out, rwkv_w0, rwkv_wA, rwkv_wB, rwkv_a0, rwkv_aA, rwkv_aB, rwkv_gA, rwkv_gB, rwkv_kk, rwkv_ka, rwkv_rk, rwkv_ln_g, diff_lq1, diff_lk1, diff_lq2, diff_lk2, diff_subln_g, lru_conv_w, lru_conv_b, lru_wa, lru_ba, lru_wx, lru_bx, lru_lambda, peer_wq, peer_keys, peer_u, peer_v, final_norm_g):
    x = jnp.concatenate([x_prompt.reshape(N_CTX, D_MODEL), x_sample.reshape(N_LAT, D_MODEL)])
    cond = jnp.concatenate([c_ctx[None], c, jnp.zeros((16 - 1 - DEC_BATCH, D_MODEL), F32)])
    cos_t, sa_t, sb_t = _rope_tables()
    head_id = jnp.arange(W_A) // HEAD_DIM
    bd = (head_id[:, None] == head_id[None, :]).astype(F32)
    zero_rwkv = jnp.zeros((2, HEAD_DIM, HEAD_DIM, BATCH * H_A), F32)
    zero_lru = jnp.zeros((BATCH, 2, W_C), F32)
    i_hi_lat = 128 // (DEC_BATCH * H_A)
    fg = final_norm_g[None]

    ks_list, vs_list, sr_list, sl_list = [], [], [], []
    for l in range(DEPTH):
        lam_init = 0.8 - 0.6 * math.exp(-0.3 * l)
        lam = (jnp.exp(jnp.sum(diff_lq1[l] * diff_lk1[l])) - jnp.exp(jnp.sum(diff_lq2[l] * diff_lk2[l]))
               + lam_init).reshape(1)
        mod = _adaln(cond, mod_w[l], mod_b[l])[:1 + DEC_BATCH].reshape(1 + DEC_BATCH, 6, D_MODEL)

        wcat = jnp.concatenate([w_in[l], rwkv_wA[l, 0], rwkv_wA[l, 1], rwkv_aA[l, 0], rwkv_aA[l, 1],
                                rwkv_gA[l]], axis=1).astype(BF16)
        (skeys, v, bonus, g, q_h, k_h, v_h, xc, gc) = _proj(
            x, mod, norm1_g[l][None], wcat,
            _block_diag(rwkv_wB[l]).astype(BF16), _block_diag(rwkv_aB[l]).astype(BF16),
            rwkv_gB[l].astype(BF16), rwkv_w0[l].reshape(1, 2 * W_A), rwkv_a0[l].reshape(1, 2 * W_A),
            rwkv_kk[l][None], rwkv_ka[l][None], rwkv_rk[l].reshape(1, W_A), bd, cos_t, sa_t, sb_t)

        o_f, o_b = [], []
        for (lo, nb, t_len, i_hi, s0) in (
                (0, BATCH, SEQ, 1, zero_rwkv),
                (N_CTX, DEC_BATCH, DEC_SEQ, i_hi_lat, _state_to_scan(state_rwkv[:, l], DEC_BATCH, i_hi_lat))):
            sl = slice(lo, lo + nb * t_len)
            of_s, ob_s, s_last = _rwkv_scan(_to_scan_keys(skeys[sl], nb, t_len, i_hi),
                                            _to_scan_vals(v[sl], nb, t_len, i_hi), s0)
            o_f.append(_from_scan_vals(of_s, nb, t_len, i_hi))
            o_b.append(_from_scan_vals(ob_s, nb, t_len, i_hi))
            if lo == 0:
                sr_list.append(_state_from_scan(s_last, nb, i_hi))
        o_f = jnp.concatenate(o_f)
        o_b = jnp.concatenate(o_b)

        sub_g = diff_subln_g[l][None]
        att_p = _attention(lam, q_h, k_h, v_h, sub_g, lam_init, nb=BATCH, t_len=SEQ, row_off=0)
        att_s = _attention(lam, q_h, k_h, v_h, sub_g, lam_init, nb=DEC_BATCH, t_len=DEC_SEQ,
                           row_off=N_CTX, layer=l, ck=cache_k, cv=cache_v)
        att = jnp.concatenate([att_p, att_s], axis=1)
        ks_list.append(k_h[:, :N_CTX].reshape(H_B, BATCH, SEQ, HEAD_DIM).transpose(1, 0, 2, 3))
        vs_list.append(v_h[:, :N_CTX].reshape(H_B, BATCH, SEQ, HEAD_DIM).transpose(1, 0, 2, 3))

        lru_args = (lru_conv_w[l], lru_conv_b[l][None], jax.vmap(_block_diag)(lru_wa[l]), lru_ba[l],
                    jax.vmap(_block_diag)(lru_wx[l]), lru_bx[l], lru_lambda[l])
        y_p, hl_p = _lru(xc, gc, *lru_args, zero_lru, nb=BATCH, t_len=SEQ, row_off=0)
        y_s, _ = _lru(xc, gc, *lru_args, state_lru[:, l], nb=DEC_BATCH, t_len=DEC_SEQ, row_off=N_CTX)
        sl_list.append(hl_p)
        y = jnp.concatenate([y_p, y_s])

        wq_hi = peer_wq[l].astype(BF16)
        wq_lo = (peer_wq[l] - wq_hi.astype(F32)).astype(BF16)
        x1, h2, q = _mid(x, mod, o_f, o_b, bonus, g, att, y, rwkv_ln_g[l][None], bd,
                         w_out[l].astype(BF16), norm2_g[l][None], wq_hi, wq_lo)
        r2, k1, e1, e2 = _peer_topk(q, peer_keys[l])
        x = _peer_dense(h2, peer_u[l].astype(BF16), peer_v[l].T.astype(BF16), r2, e2, k1, e1, x1, mod, fg,
                        final=(l == DEPTH - 1))

    y_prompt = x[:N_CTX].reshape(BATCH, SEQ, D_MODEL)
    y_sample = x[N_CTX:].reshape(DEC_BATCH, DEC_SEQ, D_MODEL)
    return (y_prompt, y_sample, jnp.stack(ks_list, axis=1), jnp.stack(vs_list, axis=1),
            jnp.stack(sr_list, axis=1), jnp.stack(sl_list, axis=1))
```
